```python
import math
import jax, jax.numpy as jnp
from jax import lax
import numpy as np

D_MODEL = 1024
BATCH = 8
SEQ = 2048
DEPTH = 1

A_WIDTH = D_MODEL // 2
A_GROUPS = 8
A_GROUP_DIM = A_WIDTH // A_GROUPS
CHUNK = 128
B_WIDTH = D_MODEL // 2
HYENA_ORDER = 2
SHORT_CONV = 3
FILTER_EMB = 33
FILTER_HIDDEN = 64
N_DIRS = 2
DECAY_TARGET = 1e-2
FAST_DECAY_PCT = 0.3
SLOW_DECAY_PCT = 1.5
DECAY_SHIFT = 0.05
N_BRANCHES = 2
D_FF = 4 * D_MODEL
EPS = 1e-6
IN_COLS = 2 * A_WIDTH + (HYENA_ORDER + 1) * B_WIDTH + N_BRANCHES * D_MODEL

kernel_name = "hybrid_gmlp_hyena_encoder_block"


def rmsnorm(x, g):
    xf = x.astype(jnp.float32)
    r = xf * lax.rsqrt(jnp.mean(xf * xf, axis=-1, keepdims=True) + EPS)
    return (r * g.astype(jnp.float32)).astype(x.dtype)


def layernorm(x, g):
    xf = x.astype(jnp.float32)
    mu = jnp.mean(xf, axis=-1, keepdims=True)
    var = jnp.mean(jnp.square(xf - mu), axis=-1, keepdims=True)
    return ((xf - mu) * lax.rsqrt(var + EPS) * g.astype(jnp.float32)).astype(x.dtype)


def spatial_gating(z, v_gain, w_s, b_s):
    u, v = jnp.split(z, 2, axis=-1)
    v = layernorm(v, v_gain)
    bsz, L, _ = v.shape
    vc = v.reshape(bsz, L // CHUNK, CHUNK, A_GROUPS, A_GROUP_DIM)
    s = jnp.einsum('gts,bcsgd->bctgd', w_s, vc) + b_s.T[None, None, :, :, None]
    return u * s.reshape(bsz, L, A_WIDTH)


def short_conv(z, w, b):
    L = z.shape[1]
    half = SHORT_CONV // 2
    zp = jnp.pad(z, ((0, 0), (half, half), (0, 0)))
    return sum(zp[:, k:k + L] * w[k] for k in range(SHORT_CONV)) + b


def hyena_filters(L, w1, b1, f1, w2, b2, f2, w3):
    f32 = jnp.float32
    t = jnp.linspace(0.0, 1.0, L, dtype=f32)[:, None]
    bands = (FILTER_EMB - 1) // 2
    w = 2.0 * math.pi * jnp.arange(L, dtype=f32)[:, None] / L
    fr = jnp.linspace(1e-4, bands - 1, bands, dtype=f32)[None, :]
    feats = jnp.concatenate([t, jnp.cos(fr * w), -jnp.sin(fr * w)], axis=-1)
    h = jnp.sin(f1.astype(f32) * (feats @ w1.astype(f32) + b1.astype(f32)))
    h = jnp.sin(f2.astype(f32) * (h @ w2.astype(f32) + b2.astype(f32)))
    h = (h @ w3.astype(f32)).reshape(L, HYENA_ORDER, N_DIRS, B_WIDTH)
    max_decay = math.log(DECAY_TARGET) / FAST_DECAY_PCT
    min_decay = math.log(DECAY_TARGET) / SLOW_DECAY_PCT
    deltas = jnp.abs(jnp.linspace(min_decay, max_decay, B_WIDTH, dtype=f32))
    decay = jnp.exp(-t[:, :, None, None] * deltas)
    h = h * (decay + DECAY_SHIFT)
    return h * lax.rsqrt(jnp.sum(h * h, axis=(0, 2), keepdims=True) + EPS)


def bidir_fft_conv(z, h_fwd, h_bwd, skip):
    L, C = h_fwd.shape
    k = jnp.concatenate([h_fwd.at[0].add(h_bwd[0]),
                         jnp.zeros((1, C), jnp.float32),
                         h_bwd[:0:-1]], axis=0)
    kf = jnp.fft.rfft(k, axis=0)
    zf32 = z.astype(jnp.float32)
    zf = jnp.fft.rfft(zf32, n=2 * L, axis=1)
    y = jnp.fft.irfft(zf * kf[None], n=2 * L, axis=1)[:, :L]
    return (y + zf32 * skip.astype(jnp.float32)).astype(z.dtype)


def hyena_mixer(p, conv_w, conv_b, w1, b1, f1, w2, b2, f2, w3, skip):
    L = p.shape[1]
    pc = short_conv(p, conv_w, conv_b)
    x1, x2, v = jnp.split(pc, HYENA_ORDER + 1, axis=-1)
    filt = hyena_filters(L, w1, b1, f1, w2, b2, f2, w3)
    z = v
    z = x1 * bidir_fft_conv(z, filt[:, 0, 0], filt[:, 0, 1], skip[0])
    z = x2 * bidir_fft_conv(z, filt[:, 1, 0], filt[:, 1, 1], skip[1])
    return z


def setup_inputs(seed: int = 0) -> dict:
    key = jax.random.key(seed)
    ks = jax.random.split(key, 32)
    nrm = lambda k, s, sc: jax.random.normal(k, s, jnp.float32) * sc
    gain = lambda k, s: 1.0 + 0.02 * jax.random.normal(k, s, jnp.float32)
    Dp = DEPTH
    return {
        "x": nrm(ks[0], (BATCH, SEQ, D_MODEL), 1.0),
        "g_pre_mix": gain(ks[1], (Dp, D_MODEL)),
        "w_in": nrm(ks[2], (Dp, D_MODEL, IN_COLS), D_MODEL ** -0.5),
        "a_v_gain": gain(ks[3], (Dp, A_WIDTH)),
        "a_w_s": nrm(ks[4], (Dp, A_GROUPS, CHUNK, CHUNK), CHUNK ** -0.5),
        "a_b_s": gain(ks[5], (Dp, A_GROUPS, CHUNK)),
        "w_out_a": nrm(ks[6], (Dp, A_WIDTH, D_MODEL), A_WIDTH ** -0.5),
        "b_conv_w": nrm(ks[7], (Dp, SHORT_CONV, (HYENA_ORDER + 1) * B_WIDTH), SHORT_CONV ** -0.5),
        "b_conv_b": nrm(ks[8], (Dp, (HYENA_ORDER + 1) * B_WIDTH), 0.02),
        "b_filt_w1": nrm(ks[9], (Dp, FILTER_EMB, FILTER_HIDDEN), FILTER_EMB ** -0.5),
        "b_filt_b1": nrm(ks[10], (Dp, FILTER_HIDDEN), 0.02),
        "b_filt_f1": gain(ks[11], (Dp, FILTER_HIDDEN)),
        "b_filt_w2": nrm(ks[12], (Dp, FILTER_HIDDEN, FILTER_HIDDEN), FILTER_HIDDEN ** -0.5),
        "b_filt_b2": nrm(ks[13], (Dp, FILTER_HIDDEN), 0.02),
        "b_filt_f2": gain(ks[14], (Dp, FILTER_HIDDEN)),
        "b_filt_w3": nrm(ks[15], (Dp, FILTER_HIDDEN, HYENA_ORDER * N_DIRS * B_WIDTH), FILTER_HIDDEN ** -0.5),
        "b_skip": nrm(ks[16], (Dp, HYENA_ORDER, B_WIDTH), 1.0),
        "w_out_b": nrm(ks[17], (Dp, B_WIDTH, D_MODEL), B_WIDTH ** -0.5),
        "w_o": nrm(ks[18], (Dp, D_MODEL, D_MODEL), D_MODEL ** -0.5),
        "g_post_mix": gain(ks[19], (Dp, D_MODEL)),
        "g_pre_ffn": gain(ks[20], (Dp, D_MODEL)),
        "w_ff1": nrm(ks[21], (Dp, D_MODEL, D_FF), D_MODEL ** -0.5),
        "w_ff2": nrm(ks[22], (Dp, D_FF, D_MODEL), D_FF ** -0.5),
        "g_post_ffn": gain(ks[23], (Dp, D_MODEL)),
    }


def reference(x, g_pre_mix, w_in, a_v_gain, a_w_s, a_b_s, w_out_a, b_conv_w, b_conv_b,
              b_filt_w1, b_filt_b1, b_filt_f1, b_filt_w2, b_filt_b2, b_filt_f2, b_filt_w3,
              b_skip, w_out_b, w_o, g_post_mix, g_pre_ffn, w_ff1, w_ff2, g_post_ffn):
    h = x
    split_a = 2 * A_WIDTH
    split_b = split_a + (HYENA_ORDER + 1) * B_WIDTH
    for i in range(DEPTH):
        xn = rmsnorm(h, g_pre_mix[i])
        p = jnp.einsum('bld,dc->blc', xn, w_in[i])
        p_a, p_b, p_g = p[..., :split_a], p[..., split_a:split_b], p[..., split_b:]
        y_a = spatial_gating(jax.nn.gelu(p_a), a_v_gain[i], a_w_s[i], a_b_s[i])
        y_a = jnp.einsum('blc,cd->bld', y_a, w_out_a[i])
        y_b = hyena_mixer(p_b, b_conv_w[i], b_conv_b[i], b_filt_w1[i], b_filt_b1[i],
                          b_filt_f1[i], b_filt_w2[i], b_filt_b2[i], b_filt_f2[i],
                          b_filt_w3[i], b_skip[i])
        y_b = jnp.einsum('blc,cd->bld', y_b, w_out_b[i])
        g_a, g_b = jnp.split(jax.nn.sigmoid(p_g), N_BRANCHES, axis=-1)
        m = jnp.einsum('bld,de->ble', g_a * y_a + g_b * y_b, w_o[i])
        h = h + rmsnorm(m, g_post_mix[i])
        hn = rmsnorm(h, g_pre_ffn[i])
        f = jnp.square(jax.nn.relu(jnp.einsum('bld,df->blf', hn, w_ff1[i])))
        f = jnp.einsum('blf,fd->bld', f, w_ff2[i])
        h = h + rmsnorm(f, g_post_ffn[i])
    return h
```

```python
import functools
import math

import jax
import jax.numpy as jnp
from jax import lax
from jax.experimental import pallas as pl
from jax.experimental.pallas import tpu as pltpu

F32 = jnp.float32
BF16 = jnp.bfloat16

D_MODEL = 1024
A_WIDTH = 512
A_GROUPS = 8
A_GROUP_DIM = A_WIDTH // A_GROUPS
CHUNK = 128
B_WIDTH = 512
HYENA_ORDER = 2
SHORT_CONV = 3
FILTER_EMB = 33
FILTER_HIDDEN = 64
DECAY_TARGET = 1e-2
FAST_DECAY_PCT = 0.3
SLOW_DECAY_PCT = 1.5
DECAY_SHIFT = 0.05
D_FF = 4 * D_MODEL
EPS = 1e-6

LANES = 128
CONV_BLOCK = 512
N_OFFSETS_HALF = 3
TOKEN_TILE = 512
CONV_CH_TILE = 256
FF_CHUNK = 1024
VMEM_LIMIT = 56 * 1024 * 1024


def _compiler_params(semantics):
    return pltpu.CompilerParams(dimension_semantics=semantics,
                                vmem_limit_bytes=VMEM_LIMIT)


def _resident(block_shape, index_map):
    return pl.BlockSpec(block_shape, index_map, pipeline_mode=pl.Buffered(1))


def _rms_scale(x):
    return lax.rsqrt(jnp.mean(x * x, axis=-1, keepdims=True) + EPS)


def _gelu_tanh(x):
    c = math.sqrt(2.0 / math.pi)
    return x * (0.5 * (1.0 + jnp.tanh(c * (x + 0.044715 * (x * x * x)))))


def _in_proj_kernel(x_ref, g_ref, w_ref, vg_ref, u_ref, v_ref, pb_ref, gate_ref):
    x = x_ref[...]
    xn = (x * _rms_scale(x) * g_ref[...]).astype(BF16)
    split_a = 2 * A_WIDTH
    split_b = split_a + (HYENA_ORDER + 1) * B_WIDTH

    pa = _gelu_tanh(jnp.dot(xn, w_ref[:, :split_a], preferred_element_type=F32))
    u_ref[...] = pa[:, :A_WIDTH].astype(BF16)
    v = pa[:, A_WIDTH:]
    mu = jnp.mean(v, axis=-1, keepdims=True)
    vc = v - mu
    var = jnp.mean(vc * vc, axis=-1, keepdims=True)
    v_ref[...] = (vc * lax.rsqrt(var + EPS) * vg_ref[...]).astype(BF16)

    pb_ref[...] = jnp.dot(xn, w_ref[:, split_a:split_b],
                          preferred_element_type=F32).astype(BF16)
    pg = jnp.dot(xn, w_ref[:, split_b:], preferred_element_type=F32)
    gate_ref[...] = jax.nn.sigmoid(pg).astype(BF16)


def _in_proj(x2, g_pre, w_in, v_gain):
    m = x2.shape[0]
    in_cols = w_in.shape[1]
    tm = TOKEN_TILE
    pb_cols = (HYENA_ORDER + 1) * B_WIDTH
    gate_cols = 2 * D_MODEL
    row = lambda i: (i, 0)
    fixed = lambda i: (0, 0)
    return pl.pallas_call(
        _in_proj_kernel,
        grid=(m // tm,),
        in_specs=[
            pl.BlockSpec((tm, D_MODEL), row),
            _resident((1, D_MODEL), fixed),
            _resident((D_MODEL, in_cols), fixed),
            _resident((1, A_WIDTH), fixed),
        ],
        out_specs=[
            pl.BlockSpec((tm, A_WIDTH), row),
            pl.BlockSpec((tm, A_WIDTH), row),
            pl.BlockSpec((tm, pb_cols), row),
            pl.BlockSpec((tm, gate_cols), row),
        ],
        out_shape=[
            jax.ShapeDtypeStruct((m, A_WIDTH), BF16),
            jax.ShapeDtypeStruct((m, A_WIDTH), BF16),
            jax.ShapeDtypeStruct((m, pb_cols), BF16),
            jax.ShapeDtypeStruct((m, gate_cols), BF16),
        ],
        compiler_params=_compiler_params(("parallel",)),
        name="in_proj",
    )(x2, g_pre, w_in, v_gain)


def _filter_kernel(feats_ref, t_ref, deltas_ref, w1_ref, b1_ref, f1_ref, w2_ref, b2_ref,
                   f2_ref, w3_ref, h_ref):
    hp = lax.Precision.HIGHEST
    h = jnp.sin(f1_ref[...] * (jnp.dot(feats_ref[...], w1_ref[...], precision=hp,
                                       preferred_element_type=F32) + b1_ref[...]))
    h = jnp.sin(f2_ref[...] * (jnp.dot(h, w2_ref[...], precision=hp,
                                       preferred_element_type=F32) + b2_ref[...]))
    h = jnp.dot(h, w3_ref[...], precision=hp, preferred_element_type=F32)
    window = jnp.exp(-t_ref[...] * deltas_ref[...]) + DECAY_SHIFT
    hf = h[:, :B_WIDTH] * window
    hb = h[:, B_WIDTH:] * window
    ss = (jnp.sum(hf * hf, axis=0, keepdims=True)
          + jnp.sum(hb * hb, axis=0, keepdims=True))
    r = lax.rsqrt(ss + EPS)
    h_ref[:, :B_WIDTH] = hf * r
    h_ref[:, B_WIDTH:] = hb * r


def _hyena_filters(seq_len, w1, b1, f1, w2, b2, f2, w3):
    t = jnp.linspace(0.0, 1.0, seq_len, dtype=F32)[:, None]
    bands = (FILTER_EMB - 1) // 2
    w = 2.0 * math.pi * jnp.arange(seq_len, dtype=F32)[:, None] / seq_len
    fr = jnp.linspace(1e-4, bands - 1, bands, dtype=F32)[None, :]
    feats = jnp.concatenate([t, jnp.cos(fr * w), -jnp.sin(fr * w)], axis=-1)
    feats = jnp.pad(feats, ((0, 0), (0, LANES - FILTER_EMB)))
    w1p = jnp.pad(w1, ((0, LANES - FILTER_EMB), (0, 0)))
    max_decay = math.log(DECAY_TARGET) / FAST_DECAY_PCT
    min_decay = math.log(DECAY_TARGET) / SLOW_DECAY_PCT
    deltas = jnp.abs(jnp.linspace(min_decay, max_decay, B_WIDTH, dtype=F32))[None, :]
    fixed = lambda o: (0, 0)
    per_order = 2 * B_WIDTH
    return pl.pallas_call(
        _filter_kernel,
        grid=(HYENA_ORDER,),
        in_specs=[
            pl.BlockSpec((seq_len, LANES), fixed),
            pl.BlockSpec((seq_len, 1), fixed),
            pl.BlockSpec((1, B_WIDTH), fixed),
            pl.BlockSpec((LANES, FILTER_HIDDEN), fixed),
            pl.BlockSpec((1, FILTER_HIDDEN), fixed),
            pl.BlockSpec((1, FILTER_HIDDEN), fixed),
            pl.BlockSpec((FILTER_HIDDEN, FILTER_HIDDEN), fixed),
            pl.BlockSpec((1, FILTER_HIDDEN), fixed),
            pl.BlockSpec((1, FILTER_HIDDEN), fixed),
            pl.BlockSpec((FILTER_HIDDEN, per_order), lambda o: (0, o)),
        ],
        out_specs=pl.BlockSpec((seq_len, per_order), lambda o: (0, o)),
        out_shape=jax.ShapeDtypeStruct((seq_len, HYENA_ORDER * per_order), F32),
        compiler_params=_compiler_params(("parallel",)),
        name="hyena_filter",
    )(feats, t, deltas, w1p, b1[None, :], f1[None, :], w2, b2[None, :], f2[None, :], w3)


def _dft_matrices():
    p = CONV_BLOCK
    n = 4 * p
    f = jnp.arange(p, dtype=jnp.int32)[:, None]
    t = jnp.arange(p, dtype=jnp.int32)[None, :]
    r = ((2 * f + 1) * t) % n
    theta = r.astype(F32) * (2.0 * math.pi / n)
    c, s = jnp.cos(theta), jnp.sin(theta)
    fwd = jnp.concatenate([c, -s], axis=0)
    inv = jnp.concatenate([c.T, -s.T], axis=1) * (1.0 / p)
    return fwd.astype(BF16), inv.astype(BF16)


def _filter_spec_kernel(hf_ref, hb_ref, fwd_ref, kr_ref, ki_ref):
    p = CONV_BLOCK
    nb = hf_ref.shape[0] // p
    fwd = fwd_ref[...]
    freq = lax.broadcasted_iota(jnp.int32, (p, hf_ref.shape[1]), 0)
    sign = jnp.where((freq & 1) == 0, 1.0, -1.0)

    def block_spectra(h_ref):
        out = []
        for m in range(nb):
            blk = h_ref[m * p:(m + 1) * p, :]
            s = jnp.dot(fwd, blk.astype(BF16), preferred_element_type=F32)
            out.append((s[:p], s[p:], blk[0:1, :]))
        return out

    sf = block_spectra(hf_ref)
    sb = block_spectra(hb_ref)
    mid = N_OFFSETS_HALF
    kr_ref[mid] = sf[0][0] + sb[0][0]
    ki_ref[mid] = sf[0][1] - sb[0][1]
    for d in range(1, nb):
        kr_ref[mid + d] = sf[d][0] - sign * sf[d - 1][1]
        ki_ref[mid + d] = sf[d][1] + sign * (sf[d - 1][0] - sf[d - 1][2])
        kr_ref[mid - d] = sb[d][0] - sign * sb[d - 1][1]
        ki_ref[mid - d] = -(sb[d][1] + sign * (sb[d - 1][0] - sb[d - 1][2]))


def _filter_spectra(h, fwd):
    seq_len = h.shape[0]
    tc = CONV_CH_TILE
    n_ct = B_WIDTH // tc
    n_off = 2 * N_OFFSETS_HALF + 1
    per_dir = B_WIDTH // tc
    out_sds = jax.ShapeDtypeStruct((HYENA_ORDER, n_off, CONV_BLOCK, B_WIDTH), F32)
    out_spec = pl.BlockSpec((None, n_off, CONV_BLOCK, tc), lambda o, c: (o, 0, 0, c))
    return pl.pallas_call(
        _filter_spec_kernel,
        grid=(HYENA_ORDER, n_ct),
        in_specs=[
            pl.BlockSpec((seq_len, tc), lambda o, c: (0, (2 * o) * per_dir + c)),
            pl.BlockSpec((seq_len, tc), lambda o, c: (0, (2 * o + 1) * per_dir + c)),
            pl.BlockSpec((2 * CONV_BLOCK, CONV_BLOCK), lambda o, c: (0, 0)),
        ],
        out_specs=[out_spec, out_spec],
        out_shape=[out_sds, out_sds],
        compiler_params=_compiler_params(("parallel", "parallel")),
        name="filter_spec",
    )(h, h, fwd)


def _short_conv(z, w_ref, b_ref):
    n = z.shape[0]
    rows = lax.broadcasted_iota(jnp.int32, z.shape, 0)
    prev = jnp.where(rows == 0, 0.0, pltpu.roll(z, 1, 0))
    nxt = jnp.where(rows == n - 1, 0.0, pltpu.roll(z, n - 1, 0))
    return prev * w_ref[0:1, :] + z * w_ref[1:2, :] + nxt * w_ref[2:3, :] + b_ref[...]


def _hyena_conv_kernel(z_ref, g_ref, cwz_ref, cbz_ref, cwg_ref, cbg_ref, skip_ref,
                       fwd_ref, inv_ref, kr_ref, ki_ref, o_ref, zf_ref, gf_ref, zs_ref,
                       *, conv_z):
    p = CONV_BLOCK
    nb = z_ref.shape[0] // p
    z = z_ref[...].astype(F32)
    if conv_z:
        z = _short_conv(z, cwz_ref, cbz_ref)
    zf_ref[...] = z
    gf_ref[...] = _short_conv(g_ref[...].astype(F32), cwg_ref, cbg_ref)

    fwd = fwd_ref[...]
    for j in range(nb):
        zs_ref[j] = jnp.dot(fwd, zf_ref[j * p:(j + 1) * p, :].astype(BF16),
                            preferred_element_type=F32)

    inv = inv_ref[...]
    skip = skip_ref[...]
    for i in range(nb):
        yr = None
        yi = None
        for j in range(nb):
            d = i - j + N_OFFSETS_HALF
            zr = zs_ref[j, :p, :]
            zi = zs_ref[j, p:, :]
            kr = kr_ref[d]
            ki = ki_ref[d]
            tr = zr * kr - zi * ki
            ti = zr * ki + zi * kr
            yr = tr if yr is None else yr + tr
            yi = ti if yi is None else yi + ti
        spec = jnp.concatenate([yr, yi], axis=0).astype(BF16)
        y = jnp.dot(inv, spec, preferred_element_type=F32)
        rows = slice(i * p, (i + 1) * p)
        zi_blk = zf_ref[rows, :]
        o_ref[rows, :] = (gf_ref[rows, :] * (y + zi_blk * skip)).astype(o_ref.dtype)


def _hyena_conv(z_arr, z_col, g_arr, g_col, conv_w, conv_b, skip, fwd, inv, kr, ki, order,
                conv_z):
    bsz, seq_len, _ = z_arr.shape
    tc = CONV_CH_TILE
    n_ct = B_WIDTH // tc
    n_off = 2 * N_OFFSETS_HALF + 1
    nb = seq_len // CONV_BLOCK
    cwz_col = z_col if conv_z else 0
    kernel = functools.partial(_hyena_conv_kernel, conv_z=conv_z)
    return pl.pallas_call(
        kernel,
        grid=(n_ct, bsz),
        in_specs=[
            pl.BlockSpec((None, seq_len, tc), lambda c, b: (b, 0, z_col + c)),
            pl.BlockSpec((None, seq_len, tc), lambda c, b: (b, 0, g_col + c)),
            pl.BlockSpec((SHORT_CONV, tc), lambda c, b: (0, cwz_col + c)),
            pl.BlockSpec((1, tc), lambda c, b: (0, cwz_col + c)),
            pl.BlockSpec((SHORT_CONV, tc), lambda c, b: (0, g_col + c)),
            pl.BlockSpec((1, tc), lambda c, b: (0, g_col + c)),
            pl.BlockSpec((None, 1, tc), lambda c, b: (order, 0, c)),
            pl.BlockSpec((2 * CONV_BLOCK, CONV_BLOCK), lambda c, b: (0, 0)),
            pl.BlockSpec((CONV_BLOCK, 2 * CONV_BLOCK), lambda c, b: (0, 0)),
            pl.BlockSpec((None, n_off, CONV_BLOCK, tc), lambda c, b: (order, 0, 0, c)),
            pl.BlockSpec((None, n_off, CONV_BLOCK, tc), lambda c, b: (order, 0, 0, c)),
        ],
        out_specs=pl.BlockSpec((None, seq_len, tc), lambda c, b: (b, 0, c)),
        out_shape=jax.ShapeDtypeStruct((bsz, seq_len, B_WIDTH), BF16),
        scratch_shapes=[
            pltpu.VMEM((seq_len, tc), F32),
            pltpu.VMEM((seq_len, tc), F32),
            pltpu.VMEM((nb, 2 * CONV_BLOCK, tc), F32),
        ],
        compiler_params=_compiler_params(("parallel", "parallel")),
        name="hyena_conv%d" % order,
    )(z_arr, g_arr, conv_w, conv_b, conv_w, conv_b, skip, fwd, inv, kr, ki)


def _mix_merge_kernel(u_ref, v_ref, yb_ref, gate_ref, x_ref, wcat_ref, sbias_ref, woa_ref,
                      wob_ref, wo_ref, g_ref, o_ref, ya_ref):
    tm = u_ref.shape[0]
    n_pairs = A_WIDTH // LANES
    lane = lax.broadcasted_iota(jnp.int32, (CHUNK, LANES), 1)
    low_half = lane < A_GROUP_DIM
    zero = jnp.zeros((CHUNK, LANES), BF16)
    for c in range(tm // CHUNK):
        rows = slice(c * CHUNK, (c + 1) * CHUNK)
        parts = []
        for j in range(n_pairs):
            vt = v_ref[rows, j * LANES:(j + 1) * LANES]
            rhs = jnp.concatenate([jnp.where(low_half, vt, zero),
                                   jnp.where(low_half, zero, vt)], axis=0)
            parts.append(jnp.dot(wcat_ref[j], rhs, preferred_element_type=F32))
        s = jnp.concatenate(parts, axis=1) + sbias_ref[...]
        ya_ref[rows, :] = (u_ref[rows, :].astype(F32) * s).astype(BF16)

    ya = jnp.dot(ya_ref[...], woa_ref[...], preferred_element_type=F32)
    yb = jnp.dot(yb_ref[...], wob_ref[...], preferred_element_type=F32)
    ga = gate_ref[:, :D_MODEL].astype(F32)
    gb = gate_ref[:, D_MODEL:].astype(F32)
    mix = (ga * ya + gb * yb).astype(BF16)
    m = jnp.dot(mix, wo_ref[...], preferred_element_type=F32)
    o_ref[...] = x_ref[...] + m * _rms_scale(m) * g_ref[...]


def _mix_merge(u, v, yb, gates, x2, wcat, sbias, w_out_a, w_out_b, w_o, g_post):
    m = x2.shape[0]
    tm = TOKEN_TILE
    row = lambda i: (i, 0)
    fixed2 = lambda i: (0, 0)
    fixed3 = lambda i: (0, 0, 0)
    return pl.pallas_call(
        _mix_merge_kernel,
        grid=(m // tm,),
        in_specs=[
            pl.BlockSpec((tm, A_WIDTH), row),
            pl.BlockSpec((tm, A_WIDTH), row),
            pl.BlockSpec((tm, B_WIDTH), row),
            pl.BlockSpec((tm, 2 * D_MODEL), row),
            pl.BlockSpec((tm, D_MODEL), row),
            _resident(wcat.shape, fixed3),
            _resident(sbias.shape, fixed2),
            _resident(w_out_a.shape, fixed2),
            _resident(w_out_b.shape, fixed2),
            _resident(w_o.shape, fixed2),
            _resident((1, D_MODEL), fixed2),
        ],
        out_specs=pl.BlockSpec((tm, D_MODEL), row),
        out_shape=jax.ShapeDtypeStruct((m, D_MODEL), F32),
        scratch_shapes=[pltpu.VMEM((tm, A_WIDTH), BF16)],
        compiler_params=_compiler_params(("parallel",)),
        name="mix_merge",
    )(u, v, yb, gates, x2, wcat, sbias, w_out_a, w_out_b, w_o, g_post)


def _ffn_kernel(h_ref, gpre_ref, w1_ref, w2_ref, gpost_ref, o_ref):
    h = h_ref[...]
    hn = (h * _rms_scale(h) * gpre_ref[...]).astype(BF16)
    acc = None
    for k in range(D_FF // FF_CHUNK):
        cols = slice(k * FF_CHUNK, (k + 1) * FF_CHUNK)
        f = jnp.dot(hn, w1_ref[:, cols], preferred_element_type=F32)
        f = jnp.square(jnp.maximum(f, 0.0)).astype(BF16)
        part = jnp.dot(f, w2_ref[cols, :], preferred_element_type=F32)
        acc = part if acc is None else acc + part
    o_ref[...] = h + acc * _rms_scale(acc) * gpost_ref[...]


def _ffn(h, g_pre, w1, w2, g_post):
    m = h.shape[0]
    tm = TOKEN_TILE
    row = lambda i: (i, 0)
    fixed = lambda i: (0, 0)
    return pl.pallas_call(
        _ffn_kernel,
        grid=(m // tm,),
        in_specs=[
            pl.BlockSpec((tm, D_MODEL), row),
            _resident((1, D_MODEL), fixed),
            _resident(w1.shape, fixed),
            _resident(w2.shape, fixed),
            _resident((1, D_MODEL), fixed),
        ],
        out_specs=pl.BlockSpec((tm, D_MODEL), row),
        out_shape=jax.ShapeDtypeStruct((m, D_MODEL), F32),
        compiler_params=_compiler_params(("parallel",)),
        name="ffn",
    )(h, g_pre, w1, w2, g_post)


def _layer(h2, bsz, seq_len, p):
    u, v, pb, gates = _in_proj(h2, p["g_pre_mix"][None, :], p["w_in"].astype(BF16),
                               p["a_v_gain"][None, :])

    filt = _hyena_filters(seq_len, p["b_filt_w1"], p["b_filt_b1"], p["b_filt_f1"],
                          p["b_filt_w2"], p["b_filt_b2"], p["b_filt_f2"], p["b_filt_w3"])
    fwd, inv = _dft_matrices()
    kr, ki = _filter_spectra(filt, fwd)

    pb3 = pb.reshape(bsz, seq_len, -1)
    conv_w = p["b_conv_w"]
    conv_b = p["b_conv_b"][None, :]
    skip = p["b_skip"][:, None, :]
    per = B_WIDTH // CONV_CH_TILE
    z1 = _hyena_conv(pb3, 2 * per, pb3, 0, conv_w, conv_b, skip, fwd, inv, kr, ki, 0, True)
    z2 = _hyena_conv(z1, 0, pb3, per, conv_w, conv_b, skip, fwd, inv, kr, ki, 1, False)
    yb = z2.reshape(bsz * seq_len, B_WIDTH)

    w_s = p["a_w_s"]
    n_pairs = A_GROUPS // 2
    wcat = jnp.concatenate([w_s[0::2], w_s[1::2]], axis=2).astype(BF16)
    assert wcat.shape == (n_pairs, CHUNK, 2 * CHUNK)
    sbias = jnp.repeat(p["a_b_s"].T, A_GROUP_DIM, axis=1)

    h_mid = _mix_merge(u, v, yb, gates, h2, wcat, sbias, p["w_out_a"].astype(BF16),
                       p["w_out_b"].astype(BF16), p["w_o"].astype(BF16),
                       p["g_post_mix"][None, :])
    return _ffn(h_mid, p["g_pre_ffn"][None, :], p["w_ff1"].astype(BF16),
                p["w_ff2"].astype(BF16), p["g_post_ffn"][None, :])


def kernel(x, g_pre_mix, w_in, a_v_gain, a_w_s, a_b_s, w_out_a, b_conv_w, b_conv_b,
           b_filt_w1, b_filt_b1, b_filt_f1, b_filt_w2, b_filt_b2, b_filt_f2, b_filt_w3,
           b_skip, w_out_b, w_o, g_post_mix, g_pre_ffn, w_ff1, w_ff2, g_post_ffn):
    params = dict(g_pre_mix=g_pre_mix, w_in=w_in, a_v_gain=a_v_gain, a_w_s=a_w_s, a_b_s=a_b_s,
                  w_out_a=w_out_a, b_conv_w=b_conv_w, b_conv_b=b_conv_b, b_filt_w1=b_filt_w1,
                  b_filt_b1=b_filt_b1, b_filt_f1=b_filt_f1, b_filt_w2=b_filt_w2,
                  b_filt_b2=b_filt_b2, b_filt_f2=b_filt_f2, b_filt_w3=b_filt_w3, b_skip=b_skip,
                  w_out_b=w_out_b, w_o=w_o, g_post_mix=g_post_mix, g_pre_ffn=g_pre_ffn,
                  w_ff1=w_ff1, w_ff2=w_ff2, g_post_ffn=g_post_ffn)
    bsz, seq_len, d = x.shape
    assert d == D_MODEL and seq_len % CONV_BLOCK == 0
    assert seq_len // CONV_BLOCK == N_OFFSETS_HALF + 1
    h2 = x.reshape(bsz * seq_len, d)
    for i in range(g_pre_mix.shape[0]):
        h2 = _layer(h2, bsz, seq_len, {k: v[i] for k, v in params.items()})
    return h2.reshape(bsz, seq_len, d)
```

```python
import functools
import math

import jax
import jax.numpy as jnp
from jax import lax
from jax.experimental import pallas as pl
from jax.experimental.pallas import tpu as pltpu

F32 = jnp.float32
BF16 = jnp.bfloat16

D_MODEL = 1024
A_WIDTH = 512
A_GROUPS = 8
A_GROUP_DIM = A_WIDTH // A_GROUPS
CHUNK = 128
B_WIDTH = 512
HYENA_ORDER = 2
SHORT_CONV = 3
FILTER_EMB = 33
FILTER_HIDDEN = 64
DECAY_TARGET = 1e-2
FAST_DECAY_PCT = 0.3
SLOW_DECAY_PCT = 1.5
DECAY_SHIFT = 0.05
D_FF = 4 * D_MODEL
EPS = 1e-6

LANES = 128
SUBLANES = 8
CONV_BLOCK = 512
N_OFFSETS_HALF = 3
TOKEN_TILE = 1024
SUB_TILE = 512
CONV_CH_TILE = 256
ROW_PHASES = 4
FF_CHUNK = 1024
VMEM_LIMIT = 56 * 1024 * 1024


def _compiler_params(semantics):
    return pltpu.CompilerParams(dimension_semantics=semantics,
                                vmem_limit_bytes=VMEM_LIMIT)


def _resident(block_shape, index_map):
    return pl.BlockSpec(block_shape, index_map, pipeline_mode=pl.Buffered(1))


def _rms_scale(x):
    return lax.rsqrt(jnp.mean(x * x, axis=-1, keepdims=True) + EPS)


def _gelu_tanh(x):
    c = math.sqrt(2.0 / math.pi)
    return x * (0.5 * (1.0 + jnp.tanh(c * (x + 0.044715 * (x * x * x)))))


def _sub_tiles(n_rows):
    return [slice(s, s + SUB_TILE) for s in range(0, n_rows, SUB_TILE)]


def _in_proj_kernel(x_ref, g_ref, w_ref, vg_ref, u_ref, v_ref, pb_ref, gate_ref):
    split_a = 2 * A_WIDTH
    split_b = split_a + (HYENA_ORDER + 1) * B_WIDTH
    for rows in _sub_tiles(x_ref.shape[0]):
        x = x_ref[rows, :]
        xn = (x * _rms_scale(x) * g_ref[...]).astype(BF16)

        pa = _gelu_tanh(jnp.dot(xn, w_ref[:, :split_a], preferred_element_type=F32))
        u_ref[rows, :] = pa[:, :A_WIDTH].astype(BF16)
        v = pa[:, A_WIDTH:]
        mu = jnp.mean(v, axis=-1, keepdims=True)
        vc = v - mu
        var = jnp.mean(vc * vc, axis=-1, keepdims=True)
        v_ref[rows, :] = (vc * lax.rsqrt(var + EPS) * vg_ref[...]).astype(BF16)

        pb_ref[rows, :] = jnp.dot(xn, w_ref[:, split_a:split_b],
                                  preferred_element_type=F32).astype(BF16)
        pg = jnp.dot(xn, w_ref[:, split_b:], preferred_element_type=F32)
        gate_ref[rows, :] = jax.nn.sigmoid(pg).astype(BF16)


def _in_proj(x2, g_pre, w_in, v_gain):
    m = x2.shape[0]
    in_cols = w_in.shape[1]
    tm = TOKEN_TILE
    pb_cols = (HYENA_ORDER + 1) * B_WIDTH
    gate_cols = 2 * D_MODEL
    row = lambda i: (i, 0)
    fixed = lambda i: (0, 0)
    return pl.pallas_call(
        _in_proj_kernel,
        grid=(m // tm,),
        in_specs=[
            pl.BlockSpec((tm, D_MODEL), row),
            _resident((1, D_MODEL), fixed),
            _resident((D_MODEL, in_cols), fixed),
            _resident((1, A_WIDTH), fixed),
        ],
        out_specs=[
            pl.BlockSpec((tm, A_WIDTH), row),
            pl.BlockSpec((tm, A_WIDTH), row),
            pl.BlockSpec((tm, pb_cols), row),
            pl.BlockSpec((tm, gate_cols), row),
        ],
        out_shape=[
            jax.ShapeDtypeStruct((m, A_WIDTH), BF16),
            jax.ShapeDtypeStruct((m, A_WIDTH), BF16),
            jax.ShapeDtypeStruct((m, pb_cols), BF16),
            jax.ShapeDtypeStruct((m, gate_cols), BF16),
        ],
        compiler_params=_compiler_params(("parallel",)),
        name="in_proj",
    )(x2, g_pre, w_in, v_gain)


def _dft_matrices():
    p = CONV_BLOCK
    n = 4 * p
    f = jnp.arange(p, dtype=jnp.int32)[:, None]
    t = jnp.arange(p, dtype=jnp.int32)[None, :]
    r = ((2 * f + 1) * t) % n
    theta = r.astype(F32) * (2.0 * math.pi / n)
    c, s = jnp.cos(theta), jnp.sin(theta)
    fwd = jnp.concatenate([c, -s], axis=0)
    inv = jnp.concatenate([c.T, -s.T], axis=1) * (1.0 / p)
    return fwd.astype(BF16), inv.astype(BF16)


def _filter_spec_kernel(feats_t_ref, w1t_ref, b1_ref, f1_ref, w2t_ref, b2_ref, f2_ref,
                        w3f_ref, w3b_ref, t_ref, deltas_ref, fwd_ref, kr_ref, ki_ref,
                        h2_ref, hf_ref, hb_ref):
    p = CONV_BLOCK
    nb = hf_ref.shape[0] // p
    hp = lax.Precision.HIGHEST

    @pl.when((pl.program_id(0) == 0) & (pl.program_id(1) == 0))
    def _():
        a = jnp.sin(f1_ref[...] * (jnp.dot(w1t_ref[...], feats_t_ref[...], precision=hp,
                                           preferred_element_type=F32) + b1_ref[...]))
        a = jnp.sin(f2_ref[...] * (jnp.dot(w2t_ref[...], a, precision=hp,
                                           preferred_element_type=F32) + b2_ref[...]))
        h2_ref[...] = a.T

    h2 = h2_ref[...]
    window = jnp.exp(-t_ref[...] * deltas_ref[...]) + DECAY_SHIFT
    hf = jnp.dot(h2, w3f_ref[...], precision=hp, preferred_element_type=F32) * window
    hb = jnp.dot(h2, w3b_ref[...], precision=hp, preferred_element_type=F32) * window
    ss = (jnp.sum(hf * hf, axis=0, keepdims=True)
          + jnp.sum(hb * hb, axis=0, keepdims=True))
    r = lax.rsqrt(ss + EPS)
    hf_ref[...] = hf * r
    hb_ref[...] = hb * r

    fwd = fwd_ref[...]
    freq = lax.broadcasted_iota(jnp.int32, (p, hf_ref.shape[1]), 0)
    sign = jnp.where((freq & 1) == 0, 1.0, -1.0)

    def block_spectra(h_ref):
        out = []
        for m in range(nb):
            blk = h_ref[m * p:(m + 1) * p, :]
            s = jnp.dot(fwd, blk.astype(BF16), preferred_element_type=F32)
            out.append((s[:p], s[p:], blk[0:1, :]))
        return out

    sf = block_spectra(hf_ref)
    sb = block_spectra(hb_ref)
    mid = N_OFFSETS_HALF
    kr_ref[mid] = sf[0][0] + sb[0][0]
    ki_ref[mid] = sf[0][1] - sb[0][1]
    for d in range(1, nb):
        kr_ref[mid + d] = sf[d][0] - sign * sf[d - 1][1]
        ki_ref[mid + d] = sf[d][1] + sign * (sf[d - 1][0] - sf[d - 1][2])
        kr_ref[mid - d] = sb[d][0] - sign * sb[d - 1][1]
        ki_ref[mid - d] = -(sb[d][1] + sign * (sb[d - 1][0] - sb[d - 1][2]))


def _filter_spectra(seq_len, w1, b1, f1, w2, b2, f2, w3, fwd):
    t = jnp.linspace(0.0, 1.0, seq_len, dtype=F32)[:, None]
    bands = (FILTER_EMB - 1) // 2
    w = 2.0 * math.pi * jnp.arange(seq_len, dtype=F32)[:, None] / seq_len
    fr = jnp.linspace(1e-4, bands - 1, bands, dtype=F32)[None, :]
    feats = jnp.concatenate([t, jnp.cos(fr * w), -jnp.sin(fr * w)], axis=-1)
    feats_t = jnp.pad(feats, ((0, 0), (0, LANES - FILTER_EMB))).T
    hid_pad = LANES - FILTER_HIDDEN
    col = lambda a: jnp.pad(a, (0, hid_pad))[:, None]
    w1t = jnp.pad(w1, ((0, LANES - FILTER_EMB), (0, hid_pad))).T
    w2t = jnp.pad(w2, ((0, hid_pad), (0, hid_pad))).T
    w3p = jnp.pad(w3, ((0, hid_pad), (0, 0)))
    max_decay = math.log(DECAY_TARGET) / FAST_DECAY_PCT
    min_decay = math.log(DECAY_TARGET) / SLOW_DECAY_PCT
    deltas = jnp.abs(jnp.linspace(min_decay, max_decay, B_WIDTH, dtype=F32))[None, :]

    tc = CONV_CH_TILE
    n_ct = B_WIDTH // tc
    n_off = 2 * N_OFFSETS_HALF + 1
    fixed = lambda o, c: (0, 0)
    out_sds = jax.ShapeDtypeStruct((HYENA_ORDER, n_off, CONV_BLOCK, B_WIDTH), F32)
    out_spec = pl.BlockSpec((None, n_off, CONV_BLOCK, tc), lambda o, c: (o, 0, 0, c))
    return pl.pallas_call(
        _filter_spec_kernel,
        grid=(HYENA_ORDER, n_ct),
        in_specs=[
            _resident((LANES, seq_len), fixed),
            _resident((LANES, LANES), fixed),
            _resident((LANES, 1), fixed),
            _resident((LANES, 1), fixed),
            _resident((LANES, LANES), fixed),
            _resident((LANES, 1), fixed),
            _resident((LANES, 1), fixed),
            pl.BlockSpec((LANES, tc), lambda o, c: (0, (2 * o) * n_ct + c)),
            pl.BlockSpec((LANES, tc), lambda o, c: (0, (2 * o + 1) * n_ct + c)),
            _resident((seq_len, 1), fixed),
            pl.BlockSpec((1, tc), lambda o, c: (0, c)),
            _resident((2 * CONV_BLOCK, CONV_BLOCK), fixed),
        ],
        out_specs=[out_spec, out_spec],
        out_shape=[out_sds, out_sds],
        scratch_shapes=[
            pltpu.VMEM((seq_len, LANES), F32),
            pltpu.VMEM((seq_len, tc), F32),
            pltpu.VMEM((seq_len, tc), F32),
        ],
        compiler_params=_compiler_params(("arbitrary", "arbitrary")),
        name="filter_spec",
    )(feats_t, w1t, col(b1), col(f1), w2t, col(b2), col(f2), w3p, w3p, t, deltas, fwd)


def _short_conv(src_ref, pad_ref, w_ref, b_ref, dst_ref):
    n = src_ref.shape[0]
    q = n // ROW_PHASES
    band = jnp.zeros((SUBLANES, LANES), F32)
    for s in range(pad_ref.shape[0]):
        lanes = slice(s * LANES, (s + 1) * LANES)
        pad_ref[s, 0:SUBLANES, :] = band
        pad_ref[s, n + SUBLANES:n + 2 * SUBLANES, :] = band
        pad_ref[s, SUBLANES:n + SUBLANES, :] = src_ref[:, lanes].astype(F32)
        w0 = w_ref[0:1, lanes]
        w1 = w_ref[1:2, lanes]
        w2 = w_ref[2:3, lanes]
        bias = b_ref[:, lanes]
        for k in range(ROW_PHASES):
            base = SUBLANES + k
            prev = pad_ref[s, pl.ds(base - 1, q, stride=ROW_PHASES), :]
            cur = pad_ref[s, pl.ds(base, q, stride=ROW_PHASES), :]
            nxt = pad_ref[s, pl.ds(base + 1, q, stride=ROW_PHASES), :]
            dst_ref[s, pl.ds(k, q, stride=ROW_PHASES), :] = (
                prev * w0 + cur * w1 + nxt * w2 + bias)


def _hyena_conv_kernel(z_ref, g_ref, cwz_ref, cbz_ref, cwg_ref, cbg_ref, skip_ref,
                       fwd_ref, inv_ref, kr_ref, ki_ref, o_ref, zf_ref, gf_ref, zs_ref,
                       pad_ref, *, conv_z):
    p = CONV_BLOCK
    nb = z_ref.shape[0] // p
    n_slabs = pad_ref.shape[0]
    if conv_z:
        _short_conv(z_ref, pad_ref, cwz_ref, cbz_ref, zf_ref)
    _short_conv(g_ref, pad_ref, cwg_ref, cbg_ref, gf_ref)

    fwd = fwd_ref[...]
    for j in range(nb):
        rows = slice(j * p, (j + 1) * p)
        if conv_z:
            zb = jnp.concatenate([zf_ref[s, rows, :] for s in range(n_slabs)],
                                 axis=1).astype(BF16)
        else:
            zb = z_ref[rows, :]
        zs_ref[j] = jnp.dot(fwd, zb, preferred_element_type=F32)

    inv = inv_ref[...]
    for i in range(nb):
        yr = None
        yi = None
        for j in range(nb):
            d = i - j + N_OFFSETS_HALF
            zr = zs_ref[j, :p, :]
            zi = zs_ref[j, p:, :]
            kr = kr_ref[d]
            ki = ki_ref[d]
            tr = zr * kr - zi * ki
            ti = zr * ki + zi * kr
            yr = tr if yr is None else yr + tr
            yi = ti if yi is None else yi + ti
        spec = jnp.concatenate([yr, yi], axis=0).astype(BF16)
        y = jnp.dot(inv, spec, preferred_element_type=F32)
        rows = slice(i * p, (i + 1) * p)
        for s in range(n_slabs):
            lanes = slice(s * LANES, (s + 1) * LANES)
            z_blk = zf_ref[s, rows, :] if conv_z else z_ref[rows, lanes].astype(F32)
            o_ref[rows, lanes] = (gf_ref[s, rows, :]
                                  * (y[:, lanes] + z_blk * skip_ref[:, lanes])
                                  ).astype(o_ref.dtype)


def _hyena_conv(z_arr, z_col, g_arr, g_col, conv_w, conv_b, skip, fwd, inv, kr, ki, order,
                conv_z):
    bsz, seq_len, _ = z_arr.shape
    tc = CONV_CH_TILE
    n_ct = B_WIDTH // tc
    n_off = 2 * N_OFFSETS_HALF + 1
    nb = seq_len // CONV_BLOCK
    n_slabs = tc // LANES
    cwz_col = z_col if conv_z else 0
    kernel = functools.partial(_hyena_conv_kernel, conv_z=conv_z)
    return pl.pallas_call(
        kernel,
        grid=(n_ct, bsz),
        in_specs=[
            pl.BlockSpec((None, seq_len, tc), lambda c, b: (b, 0, z_col + c)),
            pl.BlockSpec((None, seq_len, tc), lambda c, b: (b, 0, g_col + c)),
            pl.BlockSpec((SHORT_CONV, tc), lambda c, b: (0, cwz_col + c)),
            pl.BlockSpec((1, tc), lambda c, b: (0, cwz_col + c)),
            pl.BlockSpec((SHORT_CONV, tc), lambda c, b: (0, g_col + c)),
            pl.BlockSpec((1, tc), lambda c, b: (0, g_col + c)),
            pl.BlockSpec((None, 1, tc), lambda c, b: (order, 0, c)),
            pl.BlockSpec((2 * CONV_BLOCK, CONV_BLOCK), lambda c, b: (0, 0)),
            pl.BlockSpec((CONV_BLOCK, 2 * CONV_BLOCK), lambda c, b: (0, 0)),
            pl.BlockSpec((None, n_off, CONV_BLOCK, tc), lambda c, b: (order, 0, 0, c)),
            pl.BlockSpec((None, n_off, CONV_BLOCK, tc), lambda c, b: (order, 0, 0, c)),
        ],
        out_specs=pl.BlockSpec((None, seq_len, tc), lambda c, b: (b, 0, c)),
        out_shape=jax.ShapeDtypeStruct((bsz, seq_len, B_WIDTH), BF16),
        scratch_shapes=[
            pltpu.VMEM((n_slabs, seq_len, LANES), F32),
            pltpu.VMEM((n_slabs, seq_len, LANES), F32),
            pltpu.VMEM((nb, 2 * CONV_BLOCK, tc), F32),
            pltpu.VMEM((n_slabs, seq_len + 2 * SUBLANES, LANES), F32),
        ],
        compiler_params=_compiler_params(("parallel", "parallel")),
        name="hyena_conv%d" % order,
    )(z_arr, g_arr, conv_w, conv_b, conv_w, conv_b, skip, fwd, inv, kr, ki)


def _mix_merge_kernel(u_ref, v_ref, yb_ref, gate_ref, x_ref, wcat_ref, sbias_ref, woa_ref,
                      wob_ref, wo_ref, g_ref, o_ref, ya_ref):
    n_pairs = A_WIDTH // LANES
    lane = lax.broadcasted_iota(jnp.int32, (CHUNK, LANES), 1)
    low_half = lane < A_GROUP_DIM
    zero = jnp.zeros((CHUNK, LANES), BF16)
    for sub in _sub_tiles(u_ref.shape[0]):
        for c in range(sub.start, sub.stop, CHUNK):
            rows = slice(c, c + CHUNK)
            parts = []
            for j in range(n_pairs):
                vt = v_ref[rows, j * LANES:(j + 1) * LANES]
                rhs = jnp.concatenate([jnp.where(low_half, vt, zero),
                                       jnp.where(low_half, zero, vt)], axis=0)
                parts.append(jnp.dot(wcat_ref[j], rhs, preferred_element_type=F32))
            s = jnp.concatenate(parts, axis=1) + sbias_ref[...]
            ya_ref[rows, :] = (u_ref[rows, :].astype(F32) * s).astype(BF16)

        ya = jnp.dot(ya_ref[sub, :], woa_ref[...], preferred_element_type=F32)
        yb = jnp.dot(yb_ref[sub, :], wob_ref[...], preferred_element_type=F32)
        ga = gate_ref[sub, :D_MODEL].astype(F32)
        gb = gate_ref[sub, D_MODEL:].astype(F32)
        mix = (ga * ya + gb * yb).astype(BF16)
        m = jnp.dot(mix, wo_ref[...], preferred_element_type=F32)
        o_ref[sub, :] = x_ref[sub, :] + m * _rms_scale(m) * g_ref[...]


def _mix_merge(u, v, yb, gates, x2, wcat, sbias, w_out_a, w_out_b, w_o, g_post):
    m = x2.shape[0]
    tm = TOKEN_TILE
    row = lambda i: (i, 0)
    fixed2 = lambda i: (0, 0)
    fixed3 = lambda i: (0, 0, 0)
    return pl.pallas_call(
        _mix_merge_kernel,
        grid=(m // tm,),
        in_specs=[
            pl.BlockSpec((tm, A_WIDTH), row),
            pl.BlockSpec((tm, A_WIDTH), row),
            pl.BlockSpec((tm, B_WIDTH), row),
            pl.BlockSpec((tm, 2 * D_MODEL), row),
            pl.BlockSpec((tm, D_MODEL), row),
            _resident(wcat.shape, fixed3),
            _resident(sbias.shape, fixed2),
            _resident(w_out_a.shape, fixed2),
            _resident(w_out_b.shape, fixed2),
            _resident(w_o.shape, fixed2),
            _resident((1, D_MODEL), fixed2),
        ],
        out_specs=pl.BlockSpec((tm, D_MODEL), row),
        out_shape=jax.ShapeDtypeStruct((m, D_MODEL), F32),
        scratch_shapes=[pltpu.VMEM((tm, A_WIDTH), BF16)],
        compiler_params=_compiler_params(("parallel",)),
        name="mix_merge",
    )(u, v, yb, gates, x2, wcat, sbias, w_out_a, w_out_b, w_o, g_post)


def _ffn_kernel(h_ref, gpre_ref, w1_ref, w2_ref, gpost_ref, o_ref):
    for rows in _sub_tiles(h_ref.shape[0]):
        h = h_ref[rows, :]
        hn = (h * _rms_scale(h) * gpre_ref[...]).astype(BF16)
        acc = None
        for k in range(D_FF // FF_CHUNK):
            cols = slice(k * FF_CHUNK, (k + 1) * FF_CHUNK)
            f = jnp.dot(hn, w1_ref[:, cols], preferred_element_type=F32)
            f = jnp.square(jnp.maximum(f, 0.0)).astype(BF16)
            part = jnp.dot(f, w2_ref[cols, :], preferred_element_type=F32)
            acc = part if acc is None else acc + part
        o_ref[rows, :] = h + acc * _rms_scale(acc) * gpost_ref[...]


def _ffn(h, g_pre, w1, w2, g_post):
    m = h.shape[0]
    tm = TOKEN_TILE
    row = lambda i: (i, 0)
    fixed = lambda i: (0, 0)
    return pl.pallas_call(
        _ffn_kernel,
        grid=(m // tm,),
        in_specs=[
            pl.BlockSpec((tm, D_MODEL), row),
            _resident((1, D_MODEL), fixed),
            _resident(w1.shape, fixed),
            _resident(w2.shape, fixed),
            _resident((1, D_MODEL), fixed),
        ],
        out_specs=pl.BlockSpec((tm, D_MODEL), row),
        out_shape=jax.ShapeDtypeStruct((m, D_MODEL), F32),
        compiler_params=_compiler_params(("parallel",)),
        name="ffn",
    )(h, g_pre, w1, w2, g_post)


def _layer(h2, bsz, seq_len, p):
    u, v, pb, gates = _in_proj(h2, p["g_pre_mix"][None, :], p["w_in"].astype(BF16),
                               p["a_v_gain"][None, :])

    fwd, inv = _dft_matrices()
    kr, ki = _filter_spectra(seq_len, p["b_filt_w1"], p["b_filt_b1"], p["b_filt_f1"],
                             p["b_filt_w2"], p["b_filt_b2"], p["b_filt_f2"],
                             p["b_filt_w3"], fwd)

    pb3 = pb.reshape(bsz, seq_len, -1)
    conv_w = p["b_conv_w"]
    conv_b = p["b_conv_b"][None, :]
    skip = p["b_skip"][:, None, :]
    per = B_WIDTH // CONV_CH_TILE
    z1 = _hyena_conv(pb3, 2 * per, pb3, 0, conv_w, conv_b, skip, fwd, inv, kr, ki, 0, True)
    z2 = _hyena_conv(z1, 0, pb3, per, conv_w, conv_b, skip, fwd, inv, kr, ki, 1, False)
    yb = z2.reshape(bsz * seq_len, B_WIDTH)

    w_s = p["a_w_s"]
    n_pairs = A_GROUPS // 2
    wcat = jnp.concatenate([w_s[0::2], w_s[1::2]], axis=2).astype(BF16)
    assert wcat.shape == (n_pairs, CHUNK, 2 * CHUNK)
    sbias = jnp.repeat(p["a_b_s"].T, A_GROUP_DIM, axis=1)

    h_mid = _mix_merge(u, v, yb, gates, h2, wcat, sbias, p["w_out_a"].astype(BF16),
                       p["w_out_b"].astype(BF16), p["w_o"].astype(BF16),
                       p["g_post_mix"][None, :])
    return _ffn(h_mid, p["g_pre_ffn"][None, :], p["w_ff1"].astype(BF16),
                p["w_ff2"].astype(BF16), p["g_post_ffn"][None, :])


def kernel(x, g_pre_mix, w_in, a_v_gain, a_w_s, a_b_s, w_out_a, b_conv_w, b_conv_b,
           b_filt_w1, b_filt_b1, b_filt_f1, b_filt_w2, b_filt_b2, b_filt_f2, b_filt_w3,
           b_skip, w_out_b, w_o, g_post_mix, g_pre_ffn, w_ff1, w_ff2, g_post_ffn):
    params = dict(g_pre_mix=g_pre_mix, w_in=w_in, a_v_gain=a_v_gain, a_w_s=a_w_s, a_b_s=a_b_s,
                  w_out_a=w_out_a, b_conv_w=b_conv_w, b_conv_b=b_conv_b, b_filt_w1=b_filt_w1,
                  b_filt_b1=b_filt_b1, b_filt_f1=b_filt_f1, b_filt_w2=b_filt_w2,
                  b_filt_b2=b_filt_b2, b_filt_f2=b_filt_f2, b_filt_w3=b_filt_w3, b_skip=b_skip,
                  w_out_b=w_out_b, w_o=w_o, g_post_mix=g_post_mix, g_pre_ffn=g_pre_ffn,
                  w_ff1=w_ff1, w_ff2=w_ff2, g_post_ffn=g_post_ffn)
    bsz, seq_len, d = x.shape
    assert d == D_MODEL and seq_len % CONV_BLOCK == 0
    assert seq_len // CONV_BLOCK == N_OFFSETS_HALF + 1
    assert (bsz * seq_len) % TOKEN_TILE == 0 and seq_len % ROW_PHASES == 0
    h2 = x.reshape(bsz * seq_len, d)
    for i in range(g_pre_mix.shape[0]):
        h2 = _layer(h2, bsz, seq_len, {k: v[i] for k, v in params.items()})
    return h2.reshape(bsz, seq_len, d)
```

```python
import functools
import math

import jax
import jax.numpy as jnp
import numpy as np
from jax import lax
from jax.experimental import pallas as pl
from jax.experimental.pallas import tpu as pltpu

F32 = jnp.float32
BF16 = jnp.bfloat16

D_MODEL = 1024
A_WIDTH = 512
A_GROUPS = 8
A_GROUP_DIM = A_WIDTH // A_GROUPS
CHUNK = 128
B_WIDTH = 512
HYENA_ORDER = 2
SHORT_CONV = 3
FILTER_EMB = 33
FILTER_HIDDEN = 64
DECAY_TARGET = 1e-2
FAST_DECAY_PCT = 0.3
SLOW_DECAY_PCT = 1.5
DECAY_SHIFT = 0.05
D_FF = 4 * D_MODEL
EPS = 1e-6

LANES = 128
SUBLANES = 8
CONV_BLOCK = 512
N_OFFSETS_HALF = 3
TOKEN_TILE = 1024
SUB_TILE = 512
CONV_CH_TILE = 256
ROW_PHASES = 4
FF_CHUNK = 1024
VMEM_LIMIT = 56 * 1024 * 1024


def _compiler_params(semantics):
    return pltpu.CompilerParams(dimension_semantics=semantics,
                                vmem_limit_bytes=VMEM_LIMIT)


def _resident(block_shape, index_map):
    return pl.BlockSpec(block_shape, index_map, pipeline_mode=pl.Buffered(1))


def _rms_scale(x):
    return lax.rsqrt(jnp.mean(x * x, axis=-1, keepdims=True) + EPS)


def _gelu_tanh(x):
    c = math.sqrt(2.0 / math.pi)
    return x * (0.5 * (1.0 + jnp.tanh(c * (x + 0.044715 * (x * x * x)))))


def _sub_tiles(n_rows):
    return [slice(s, s + SUB_TILE) for s in range(0, n_rows, SUB_TILE)]


def _in_proj_kernel(x_ref, g_ref, w_ref, vg_ref, u_ref, v_ref, pb_ref, gate_ref):
    split_a = 2 * A_WIDTH
    split_b = split_a + (HYENA_ORDER + 1) * B_WIDTH
    for rows in _sub_tiles(x_ref.shape[0]):
        x = x_ref[rows, :]
        xn = (x * _rms_scale(x) * g_ref[...]).astype(BF16)

        pa = _gelu_tanh(jnp.dot(xn, w_ref[:, :split_a], preferred_element_type=F32))
        u_ref[rows, :] = pa[:, :A_WIDTH].astype(BF16)
        v = pa[:, A_WIDTH:]
        mu = jnp.mean(v, axis=-1, keepdims=True)
        vc = v - mu
        var = jnp.mean(vc * vc, axis=-1, keepdims=True)
        v_ref[rows, :] = (vc * lax.rsqrt(var + EPS) * vg_ref[...]).astype(BF16)

        pb_ref[rows, :] = jnp.dot(xn, w_ref[:, split_a:split_b],
                                  preferred_element_type=F32).astype(BF16)
        pg = jnp.dot(xn, w_ref[:, split_b:], preferred_element_type=F32)
        gate_ref[rows, :] = jax.nn.sigmoid(pg).astype(BF16)


def _in_proj(x2, g_pre, w_in, v_gain):
    m = x2.shape[0]
    in_cols = w_in.shape[1]
    tm = TOKEN_TILE
    pb_cols = (HYENA_ORDER + 1) * B_WIDTH
    gate_cols = 2 * D_MODEL
    row = lambda i: (i, 0)
    fixed = lambda i: (0, 0)
    return pl.pallas_call(
        _in_proj_kernel,
        grid=(m // tm,),
        in_specs=[
            pl.BlockSpec((tm, D_MODEL), row),
            _resident((1, D_MODEL), fixed),
            _resident((D_MODEL, in_cols), fixed),
            _resident((1, A_WIDTH), fixed),
        ],
        out_specs=[
            pl.BlockSpec((tm, A_WIDTH), row),
            pl.BlockSpec((tm, A_WIDTH), row),
            pl.BlockSpec((tm, pb_cols), row),
            pl.BlockSpec((tm, gate_cols), row),
        ],
        out_shape=[
            jax.ShapeDtypeStruct((m, A_WIDTH), BF16),
            jax.ShapeDtypeStruct((m, A_WIDTH), BF16),
            jax.ShapeDtypeStruct((m, pb_cols), BF16),
            jax.ShapeDtypeStruct((m, gate_cols), BF16),
        ],
        compiler_params=_compiler_params(("parallel",)),
        name="in_proj",
    )(x2, g_pre, w_in, v_gain)


def _dft_matrices():
    p = CONV_BLOCK
    n = 4 * p
    f = np.arange(p, dtype=np.int64)[:, None]
    t = np.arange(p, dtype=np.int64)[None, :]
    theta = (((2 * f + 1) * t) % n).astype(np.float64) * (2.0 * math.pi / n)
    c, s = np.cos(theta), np.sin(theta)
    fwd = np.concatenate([c, -s], axis=0).astype(np.float32)
    inv = (np.concatenate([c.T, -s.T], axis=1) * (1.0 / p)).astype(np.float32)
    return jnp.asarray(fwd).astype(BF16), jnp.asarray(inv).astype(BF16)


def _filter_spec_kernel(feats_t_ref, w1t_ref, b1_ref, f1_ref, w2t_ref, b2_ref, f2_ref,
                        w3f_ref, w3b_ref, t_ref, deltas_ref, fwd_ref, kr_ref, ki_ref,
                        h2t_ref, hf_ref, hb_ref):
    p = CONV_BLOCK
    nb = hf_ref.shape[0] // p
    hp = lax.Precision.HIGHEST

    @pl.when((pl.program_id(0) == 0) & (pl.program_id(1) == 0))
    def _():
        a = jnp.sin(f1_ref[...] * (jnp.dot(w1t_ref[...], feats_t_ref[...], precision=hp,
                                           preferred_element_type=F32) + b1_ref[...]))
        h2t_ref[...] = jnp.sin(f2_ref[...] * (jnp.dot(w2t_ref[...], a, precision=hp,
                                                      preferred_element_type=F32)
                                              + b2_ref[...]))

    h2t = h2t_ref[...]
    h_hi = h2t.astype(BF16)
    h_lo = (h2t - h_hi.astype(F32)).astype(BF16)
    h_stack = jnp.concatenate([h_hi, h_hi, h_lo], axis=0)
    over_hidden = (((0,), (0,)), ((), ()))

    def last(w_ref):
        w = w_ref[...]
        w_hi = w.astype(BF16)
        w_lo = (w - w_hi.astype(F32)).astype(BF16)
        w_stack = jnp.concatenate([w_hi, w_lo, w_hi], axis=0)
        return lax.dot_general(h_stack, w_stack, over_hidden, preferred_element_type=F32)

    window = jnp.exp(-t_ref[...] * deltas_ref[...]) + DECAY_SHIFT
    hf = last(w3f_ref) * window
    hb = last(w3b_ref) * window
    ss = (jnp.sum(hf * hf, axis=0, keepdims=True)
          + jnp.sum(hb * hb, axis=0, keepdims=True))
    r = lax.rsqrt(ss + EPS)
    hf_ref[...] = hf * r
    hb_ref[...] = hb * r

    fwd = fwd_ref[...]
    freq = lax.broadcasted_iota(jnp.int32, (p, hf_ref.shape[1]), 0)
    sign = jnp.where((freq & 1) == 0, 1.0, -1.0)

    def block_spectra(h_ref):
        out = []
        for m in range(nb):
            blk = h_ref[m * p:(m + 1) * p, :]
            s = jnp.dot(fwd, blk.astype(BF16), preferred_element_type=F32)
            out.append((s[:p], s[p:], blk[0:1, :]))
        return out

    sf = block_spectra(hf_ref)
    sb = block_spectra(hb_ref)
    mid = N_OFFSETS_HALF
    kr_ref[mid] = sf[0][0] + sb[0][0]
    ki_ref[mid] = sf[0][1] - sb[0][1]
    for d in range(1, nb):
        kr_ref[mid + d] = sf[d][0] - sign * sf[d - 1][1]
        ki_ref[mid + d] = sf[d][1] + sign * (sf[d - 1][0] - sf[d - 1][2])
        kr_ref[mid - d] = sb[d][0] - sign * sb[d - 1][1]
        ki_ref[mid - d] = -(sb[d][1] + sign * (sb[d - 1][0] - sb[d - 1][2]))


def _filter_spectra(seq_len, w1, b1, f1, w2, b2, f2, w3, fwd):
    t = jnp.linspace(0.0, 1.0, seq_len, dtype=F32)[:, None]
    bands = (FILTER_EMB - 1) // 2
    w = 2.0 * math.pi * jnp.arange(seq_len, dtype=F32)[:, None] / seq_len
    fr = jnp.linspace(1e-4, bands - 1, bands, dtype=F32)[None, :]
    feats = jnp.concatenate([t, jnp.cos(fr * w), -jnp.sin(fr * w)], axis=-1)
    feats_t = jnp.pad(feats, ((0, 0), (0, LANES - FILTER_EMB))).T
    col = lambda a: a[:, None]
    w1t = jnp.pad(w1, ((0, LANES - FILTER_EMB), (0, 0))).T
    w2t = w2.T
    max_decay = math.log(DECAY_TARGET) / FAST_DECAY_PCT
    min_decay = math.log(DECAY_TARGET) / SLOW_DECAY_PCT
    deltas = jnp.abs(jnp.linspace(min_decay, max_decay, B_WIDTH, dtype=F32))[None, :]

    tc = CONV_CH_TILE
    n_ct = B_WIDTH // tc
    n_off = 2 * N_OFFSETS_HALF + 1
    fixed = lambda o, c: (0, 0)
    out_sds = jax.ShapeDtypeStruct((HYENA_ORDER, n_off, CONV_BLOCK, B_WIDTH), F32)
    out_spec = pl.BlockSpec((None, n_off, CONV_BLOCK, tc), lambda o, c: (o, 0, 0, c))
    return pl.pallas_call(
        _filter_spec_kernel,
        grid=(HYENA_ORDER, n_ct),
        in_specs=[
            _resident((LANES, seq_len), fixed),
            _resident((FILTER_HIDDEN, LANES), fixed),
            _resident((FILTER_HIDDEN, 1), fixed),
            _resident((FILTER_HIDDEN, 1), fixed),
            _resident((FILTER_HIDDEN, FILTER_HIDDEN), fixed),
            _resident((FILTER_HIDDEN, 1), fixed),
            _resident((FILTER_HIDDEN, 1), fixed),
            pl.BlockSpec((FILTER_HIDDEN, tc), lambda o, c: (0, (2 * o) * n_ct + c)),
            pl.BlockSpec((FILTER_HIDDEN, tc), lambda o, c: (0, (2 * o + 1) * n_ct + c)),
            _resident((seq_len, 1), fixed),
            pl.BlockSpec((1, tc), lambda o, c: (0, c)),
            _resident((2 * CONV_BLOCK, CONV_BLOCK), fixed),
        ],
        out_specs=[out_spec, out_spec],
        out_shape=[out_sds, out_sds],
        scratch_shapes=[
            pltpu.VMEM((FILTER_HIDDEN, seq_len), F32),
            pltpu.VMEM((seq_len, tc), F32),
            pltpu.VMEM((seq_len, tc), F32),
        ],
        compiler_params=_compiler_params(("arbitrary", "arbitrary")),
        name="filter_spec",
    )(feats_t, w1t, col(b1), col(f1), w2t, col(b2), col(f2), w3, w3, t, deltas, fwd)


def _short_conv(src_ref, pad_ref, w_ref, b_ref, dst_ref):
    n = src_ref.shape[0]
    q = n // ROW_PHASES
    band = jnp.zeros((SUBLANES, LANES), F32)
    for s in range(pad_ref.shape[0]):
        lanes = slice(s * LANES, (s + 1) * LANES)
        pad_ref[s, 0:SUBLANES, :] = band
        pad_ref[s, n + SUBLANES:n + 2 * SUBLANES, :] = band
        pad_ref[s, SUBLANES:n + SUBLANES, :] = src_ref[:, lanes].astype(F32)
        w0 = w_ref[0:1, lanes]
        w1 = w_ref[1:2, lanes]
        w2 = w_ref[2:3, lanes]
        bias = b_ref[:, lanes]
        for k in range(ROW_PHASES):
            base = SUBLANES + k
            prev = pad_ref[s, pl.ds(base - 1, q, stride=ROW_PHASES), :]
            cur = pad_ref[s, pl.ds(base, q, stride=ROW_PHASES), :]
            nxt = pad_ref[s, pl.ds(base + 1, q, stride=ROW_PHASES), :]
            dst_ref[s, pl.ds(k, q, stride=ROW_PHASES), :] = (
                prev * w0 + cur * w1 + nxt * w2 + bias)


def _hyena_conv_kernel(z_ref, g_ref, cwz_ref, cbz_ref, cwg_ref, cbg_ref, skip_ref,
                       fwd_ref, inv_ref, kr_ref, ki_ref, o_ref, zf_ref, gf_ref, zs_ref,
                       pad_ref, *, conv_z):
    p = CONV_BLOCK
    nb = z_ref.shape[0] // p
    n_slabs = pad_ref.shape[0]
    if conv_z:
        _short_conv(z_ref, pad_ref, cwz_ref, cbz_ref, zf_ref)
    _short_conv(g_ref, pad_ref, cwg_ref, cbg_ref, gf_ref)

    fwd = fwd_ref[...]
    for j in range(nb):
        rows = slice(j * p, (j + 1) * p)
        if conv_z:
            zb = jnp.concatenate([zf_ref[s, rows, :] for s in range(n_slabs)],
                                 axis=1).astype(BF16)
        else:
            zb = z_ref[rows, :]
        zs_ref[j] = jnp.dot(fwd, zb, preferred_element_type=F32)

    inv = inv_ref[...]
    for i in range(nb):
        yr = None
        yi = None
        for j in range(nb):
            d = i - j + N_OFFSETS_HALF
            zr = zs_ref[j, :p, :]
            zi = zs_ref[j, p:, :]
            kr = kr_ref[d]
            ki = ki_ref[d]
            tr = zr * kr - zi * ki
            ti = zr * ki + zi * kr
            yr = tr if yr is None else yr + tr
            yi = ti if yi is None else yi + ti
        spec = jnp.concatenate([yr, yi], axis=0).astype(BF16)
        y = jnp.dot(inv, spec, preferred_element_type=F32)
        rows = slice(i * p, (i + 1) * p)
        for s in range(n_slabs):
            lanes = slice(s * LANES, (s + 1) * LANES)
            z_blk = zf_ref[s, rows, :] if conv_z else z_ref[rows, lanes].astype(F32)
            o_ref[rows, lanes] = (gf_ref[s, rows, :]
                                  * (y[:, lanes] + z_blk * skip_ref[:, lanes])
                                  ).astype(o_ref.dtype)


def _hyena_conv(z_arr, z_col, g_arr, g_col, conv_w, conv_b, skip, fwd, inv, kr, ki, order,
                conv_z):
    bsz, seq_len, _ = z_arr.shape
    tc = CONV_CH_TILE
    n_ct = B_WIDTH // tc
    n_off = 2 * N_OFFSETS_HALF + 1
    nb = seq_len // CONV_BLOCK
    n_slabs = tc // LANES
    cwz_col = z_col if conv_z else 0
    kernel = functools.partial(_hyena_conv_kernel, conv_z=conv_z)
    return pl.pallas_call(
        kernel,
        grid=(n_ct, bsz),
        in_specs=[
            pl.BlockSpec((None, seq_len, tc), lambda c, b: (b, 0, z_col + c)),
            pl.BlockSpec((None, seq_len, tc), lambda c, b: (b, 0, g_col + c)),
            pl.BlockSpec((SHORT_CONV, tc), lambda c, b: (0, cwz_col + c)),
            pl.BlockSpec((1, tc), lambda c, b: (0, cwz_col + c)),
            pl.BlockSpec((SHORT_CONV, tc), lambda c, b: (0, g_col + c)),
            pl.BlockSpec((1, tc), lambda c, b: (0, g_col + c)),
            pl.BlockSpec((None, 1, tc), lambda c, b: (order, 0, c)),
            pl.BlockSpec((2 * CONV_BLOCK, CONV_BLOCK), lambda c, b: (0, 0)),
            pl.BlockSpec((CONV_BLOCK, 2 * CONV_BLOCK), lambda c, b: (0, 0)),
            pl.BlockSpec((None, n_off, CONV_BLOCK, tc), lambda c, b: (order, 0, 0, c)),
            pl.BlockSpec((None, n_off, CONV_BLOCK, tc), lambda c, b: (order, 0, 0, c)),
        ],
        out_specs=pl.BlockSpec((None, seq_len, tc), lambda c, b: (b, 0, c)),
        out_shape=jax.ShapeDtypeStruct((bsz, seq_len, B_WIDTH), BF16),
        scratch_shapes=[
            pltpu.VMEM((n_slabs, seq_len, LANES), F32),
            pltpu.VMEM((n_slabs, seq_len, LANES), F32),
            pltpu.VMEM((nb, 2 * CONV_BLOCK, tc), F32),
            pltpu.VMEM((n_slabs, seq_len + 2 * SUBLANES, LANES), F32),
        ],
        compiler_params=_compiler_params(("parallel", "parallel")),
        name="hyena_conv%d" % order,
    )(z_arr, g_arr, conv_w, conv_b, conv_w, conv_b, skip, fwd, inv, kr, ki)


def _mix_merge_kernel(u_ref, v_ref, yb_ref, gate_ref, x_ref, wcat_ref, sbias_ref, woa_ref,
                      wob_ref, wo_ref, g_ref, o_ref, ya_ref):
    n_pairs = A_WIDTH // LANES
    lane = lax.broadcasted_iota(jnp.int32, (CHUNK, LANES), 1)
    low_half = lane < A_GROUP_DIM
    zero = jnp.zeros((CHUNK, LANES), BF16)
    for sub in _sub_tiles(u_ref.shape[0]):
        for c in range(sub.start, sub.stop, CHUNK):
            rows = slice(c, c + CHUNK)
            parts = []
            for j in range(n_pairs):
                vt = v_ref[rows, j * LANES:(j + 1) * LANES]
                rhs = jnp.concatenate([jnp.where(low_half, vt, zero),
                                       jnp.where(low_half, zero, vt)], axis=0)
                parts.append(jnp.dot(wcat_ref[j], rhs, preferred_element_type=F32))
            s = jnp.concatenate(parts, axis=1) + sbias_ref[...]
            ya_ref[rows, :] = (u_ref[rows, :].astype(F32) * s).astype(BF16)

        ya = jnp.dot(ya_ref[sub, :], woa_ref[...], preferred_element_type=F32)
        yb = jnp.dot(yb_ref[sub, :], wob_ref[...], preferred_element_type=F32)
        ga = gate_ref[sub, :D_MODEL].astype(F32)
        gb = gate_ref[sub, D_MODEL:].astype(F32)
        mix = (ga * ya + gb * yb).astype(BF16)
        m = jnp.dot(mix, wo_ref[...], preferred_element_type=F32)
        o_ref[sub, :] = x_ref[sub, :] + m * _rms_scale(m) * g_ref[...]


def _mix_merge(u, v, yb, gates, x2, wcat, sbias, w_out_a, w_out_b, w_o, g_post):
    m = x2.shape[0]
    tm = TOKEN_TILE
    row = lambda i: (i, 0)
    fixed2 = lambda i: (0, 0)
    fixed3 = lambda i: (0, 0, 0)
    return pl.pallas_call(
        _mix_merge_kernel,
        grid=(m // tm,),
        in_specs=[
            pl.BlockSpec((tm, A_WIDTH), row),
            pl.BlockSpec((tm, A_WIDTH), row),
            pl.BlockSpec((tm, B_WIDTH), row),
            pl.BlockSpec((tm, 2 * D_MODEL), row),
            pl.BlockSpec((tm, D_MODEL), row),
            _resident(wcat.shape, fixed3),
            _resident(sbias.shape, fixed2),
            _resident(w_out_a.shape, fixed2),
            _resident(w_out_b.shape, fixed2),
            _resident(w_o.shape, fixed2),
            _resident((1, D_MODEL), fixed2),
        ],
        out_specs=pl.BlockSpec((tm, D_MODEL), row),
        out_shape=jax.ShapeDtypeStruct((m, D_MODEL), F32),
        scratch_shapes=[pltpu.VMEM((tm, A_WIDTH), BF16)],
        compiler_params=_compiler_params(("parallel",)),
        name="mix_merge",
    )(u, v, yb, gates, x2, wcat, sbias, w_out_a, w_out_b, w_o, g_post)


def _ffn_kernel(h_ref, gpre_ref, w1_ref, w2_ref, gpost_ref, o_ref):
    for rows in _sub_tiles(h_ref.shape[0]):
        h = h_ref[rows, :]
        hn = (h * _rms_scale(h) * gpre_ref[...]).astype(BF16)
        acc = None
        for k in range(D_FF // FF_CHUNK):
            cols = slice(k * FF_CHUNK, (k + 1) * FF_CHUNK)
            f = jnp.dot(hn, w1_ref[:, cols], preferred_element_type=F32)
            f = jnp.square(jnp.maximum(f, 0.0)).astype(BF16)
            part = jnp.dot(f, w2_ref[cols, :], preferred_element_type=F32)
            acc = part if acc is None else acc + part
        o_ref[rows, :] = h + acc * _rms_scale(acc) * gpost_ref[...]


def _ffn(h, g_pre, w1, w2, g_post):
    m = h.shape[0]
    tm = TOKEN_TILE
    row = lambda i: (i, 0)
    fixed = lambda i: (0, 0)
    return pl.pallas_call(
        _ffn_kernel,
        grid=(m // tm,),
        in_specs=[
            pl.BlockSpec((tm, D_MODEL), row),
            _resident((1, D_MODEL), fixed),
            _resident(w1.shape, fixed),
            _resident(w2.shape, fixed),
            _resident((1, D_MODEL), fixed),
        ],
        out_specs=pl.BlockSpec((tm, D_MODEL), row),
        out_shape=jax.ShapeDtypeStruct((m, D_MODEL), F32),
        compiler_params=_compiler_params(("parallel",)),
        name="ffn",
    )(h, g_pre, w1, w2, g_post)


def _layer(h2, bsz, seq_len, p):
    u, v, pb, gates = _in_proj(h2, p["g_pre_mix"][None, :], p["w_in"].astype(BF16),
                               p["a_v_gain"][None, :])

    fwd, inv = _dft_matrices()
    kr, ki = _filter_spectra(seq_len, p["b_filt_w1"], p["b_filt_b1"], p["b_filt_f1"],
                             p["b_filt_w2"], p["b_filt_b2"], p["b_filt_f2"],
                             p["b_filt_w3"], fwd)

    pb3 = pb.reshape(bsz, seq_len, -1)
    conv_w = p["b_conv_w"]
    conv_b = p["b_conv_b"][None, :]
    skip = p["b_skip"][:, None, :]
    per = B_WIDTH // CONV_CH_TILE
    z1 = _hyena_conv(pb3, 2 * per, pb3, 0, conv_w, conv_b, skip, fwd, inv, kr, ki, 0, True)
    z2 = _hyena_conv(z1, 0, pb3, per, conv_w, conv_b, skip, fwd, inv, kr, ki, 1, False)
    yb = z2.reshape(bsz * seq_len, B_WIDTH)

    w_s = p["a_w_s"]
    n_pairs = A_GROUPS // 2
    wcat = jnp.concatenate([w_s[0::2], w_s[1::2]], axis=2).astype(BF16)
    assert wcat.shape == (n_pairs, CHUNK, 2 * CHUNK)
    sbias = jnp.repeat(p["a_b_s"].T, A_GROUP_DIM, axis=1)

    h_mid = _mix_merge(u, v, yb, gates, h2, wcat, sbias, p["w_out_a"].astype(BF16),
                       p["w_out_b"].astype(BF16), p["w_o"].astype(BF16),
                       p["g_post_mix"][None, :])
    return _ffn(h_mid, p["g_pre_ffn"][None, :], p["w_ff1"].astype(BF16),
                p["w_ff2"].astype(BF16), p["g_post_ffn"][None, :])


def kernel(x, g_pre_mix, w_in, a_v_gain, a_w_s, a_b_s, w_out_a, b_conv_w, b_conv_b,
           b_filt_w1, b_filt_b1, b_filt_f1, b_filt_w2, b_filt_b2, b_filt_f2, b_filt_w3,
           b_skip, w_out_b, w_o, g_post_mix, g_pre_ffn, w_ff1, w_ff2, g_post_ffn):
    params = dict(g_pre_mix=g_pre_mix, w_in=w_in, a_v_gain=a_v_gain, a_w_s=a_w_s, a_b_s=a_b_s,
                  w_out_a=w_out_a, b_conv_w=b_conv_w, b_conv_b=b_conv_b, b_filt_w1=b_filt_w1,
                  b_filt_b1=b_filt_b1, b_filt_f1=b_filt_f1, b_filt_w2=b_filt_w2,
                  b_filt_b2=b_filt_b2, b_filt_f2=b_filt_f2, b_filt_w3=b_filt_w3, b_skip=b_skip,
                  w_out_b=w_out_b, w_o=w_o, g_post_mix=g_post_mix, g_pre_ffn=g_pre_ffn,
                  w_ff1=w_ff1, w_ff2=w_ff2, g_post_ffn=g_post_ffn)
    bsz, seq_len, d = x.shape
    assert d == D_MODEL and seq_len % CONV_BLOCK == 0
    assert seq_len // CONV_BLOCK == N_OFFSETS_HALF + 1
    assert (bsz * seq_len) % TOKEN_TILE == 0 and seq_len % ROW_PHASES == 0
    h2 = x.reshape(bsz * seq_len, d)
    for i in range(g_pre_mix.shape[0]):
        h2 = _layer(h2, bsz, seq_len, {k: v[i] for k, v in params.items()})
    return h2.reshape(bsz, seq_len, d)
```

```python
import functools
import math

import jax
import jax.numpy as jnp
import numpy as np
from jax import lax
from jax.experimental import pallas as pl
from jax.experimental.pallas import tpu as pltpu

F32 = jnp.float32
BF16 = jnp.bfloat16

D_MODEL = 1024
A_WIDTH = 512
A_GROUPS = 8
A_GROUP_DIM = A_WIDTH // A_GROUPS
CHUNK = 128
B_WIDTH = 512
HYENA_ORDER = 2
SHORT_CONV = 3
FILTER_EMB = 33
FILTER_HIDDEN = 64
DECAY_TARGET = 1e-2
FAST_DECAY_PCT = 0.3
SLOW_DECAY_PCT = 1.5
DECAY_SHIFT = 0.05
D_FF = 4 * D_MODEL
EPS = 1e-6

LANES = 128
SUBLANES = 8
CONV_BLOCK = 512
N_OFFSETS_HALF = 3
TOKEN_TILE = 1024
MIX_TILE = 512
SUB_TILE = 512
CONV_CH_TILE = 256
ROW_PHASES = 4
FF_CHUNK = 1024
VMEM_LIMIT = 56 * 1024 * 1024


def _compiler_params(semantics):
    return pltpu.CompilerParams(dimension_semantics=semantics,
                                vmem_limit_bytes=VMEM_LIMIT)


def _resident(block_shape, index_map):
    return pl.BlockSpec(block_shape, index_map, pipeline_mode=pl.Buffered(1))


def _rms_scale(x):
    return lax.rsqrt(jnp.mean(x * x, axis=-1, keepdims=True) + EPS)


def _gelu_tanh(x):
    c = math.sqrt(2.0 / math.pi)
    return x * (0.5 * (1.0 + jnp.tanh(c * (x + 0.044715 * (x * x * x)))))


def _sub_tiles(n_rows):
    return [slice(s, s + SUB_TILE) for s in range(0, n_rows, SUB_TILE)]


def _in_proj_kernel(x_ref, g_ref, w_ref, vg_ref, u_ref, v_ref, pb_ref, gate_ref):
    split_a = 2 * A_WIDTH
    split_b = split_a + (HYENA_ORDER + 1) * B_WIDTH
    for rows in _sub_tiles(x_ref.shape[0]):
        x = x_ref[rows, :]
        xn = (x * _rms_scale(x) * g_ref[...]).astype(BF16)

        pa = _gelu_tanh(jnp.dot(xn, w_ref[:, :split_a], preferred_element_type=F32))
        u_ref[rows, :] = pa[:, :A_WIDTH].astype(BF16)
        v = pa[:, A_WIDTH:]
        mu = jnp.mean(v, axis=-1, keepdims=True)
        vc = v - mu
        var = jnp.mean(vc * vc, axis=-1, keepdims=True)
        v_ref[rows, :] = (vc * lax.rsqrt(var + EPS) * vg_ref[...]).astype(BF16)

        pb_ref[rows, :] = jnp.dot(xn, w_ref[:, split_a:split_b],
                                  preferred_element_type=F32).astype(BF16)
        pg = jnp.dot(xn, w_ref[:, split_b:], preferred_element_type=F32)
        gate_ref[rows, :] = jax.nn.sigmoid(pg).astype(BF16)


def _in_proj(x2, g_pre, w_in, v_gain):
    m = x2.shape[0]
    in_cols = w_in.shape[1]
    tm = TOKEN_TILE
    pb_cols = (HYENA_ORDER + 1) * B_WIDTH
    gate_cols = 2 * D_MODEL
    row = lambda i: (i, 0)
    fixed = lambda i: (0, 0)
    return pl.pallas_call(
        _in_proj_kernel,
        grid=(m // tm,),
        in_specs=[
            pl.BlockSpec((tm, D_MODEL), row),
            _resident((1, D_MODEL), fixed),
            _resident((D_MODEL, in_cols), fixed),
            _resident((1, A_WIDTH), fixed),
        ],
        out_specs=[
            pl.BlockSpec((tm, A_WIDTH), row),
            pl.BlockSpec((tm, A_WIDTH), row),
            pl.BlockSpec((tm, pb_cols), row),
            pl.BlockSpec((tm, gate_cols), row),
        ],
        out_shape=[
            jax.ShapeDtypeStruct((m, A_WIDTH), BF16),
            jax.ShapeDtypeStruct((m, A_WIDTH), BF16),
            jax.ShapeDtypeStruct((m, pb_cols), BF16),
            jax.ShapeDtypeStruct((m, gate_cols), BF16),
        ],
        compiler_params=_compiler_params(("parallel",)),
        name="in_proj",
    )(x2, g_pre, w_in, v_gain)


def _dft_matrices():
    p = CONV_BLOCK
    n = 4 * p
    f = np.arange(p, dtype=np.int64)[:, None]
    t = np.arange(p, dtype=np.int64)[None, :]
    theta = (((2 * f + 1) * t) % n).astype(np.float64) * (2.0 * math.pi / n)
    c, s = np.cos(theta), np.sin(theta)
    fwd = np.concatenate([c, -s], axis=0).astype(np.float32)
    inv = (np.concatenate([c.T, -s.T], axis=1) * (1.0 / p)).astype(np.float32)
    return jnp.asarray(fwd).astype(BF16), jnp.asarray(inv).astype(BF16)


def _filter_spec_kernel(feats_t_ref, w1t_ref, b1_ref, f1_ref, w2t_ref, b2_ref, f2_ref,
                        w3f_ref, w3b_ref, t_ref, deltas_ref, fwd_ref, kr_ref, ki_ref,
                        h2t_ref, hf_ref, hb_ref):
    p = CONV_BLOCK
    nb = hf_ref.shape[0] // p
    hp = lax.Precision.HIGHEST

    @pl.when((pl.program_id(0) == 0) & (pl.program_id(1) == 0))
    def _():
        a = jnp.sin(f1_ref[...] * (jnp.dot(w1t_ref[...], feats_t_ref[...], precision=hp,
                                           preferred_element_type=F32) + b1_ref[...]))
        h2t_ref[...] = jnp.sin(f2_ref[...] * (jnp.dot(w2t_ref[...], a, precision=hp,
                                                      preferred_element_type=F32)
                                              + b2_ref[...]))

    h2t = h2t_ref[...]
    h_hi = h2t.astype(BF16)
    h_lo = (h2t - h_hi.astype(F32)).astype(BF16)
    h_stack = jnp.concatenate([h_hi, h_hi, h_lo], axis=0)
    over_hidden = (((0,), (0,)), ((), ()))

    def last(w_ref):
        w = w_ref[...]
        w_hi = w.astype(BF16)
        w_lo = (w - w_hi.astype(F32)).astype(BF16)
        w_stack = jnp.concatenate([w_hi, w_lo, w_hi], axis=0)
        return lax.dot_general(h_stack, w_stack, over_hidden, preferred_element_type=F32)

    window = jnp.exp(-t_ref[...] * deltas_ref[...]) + DECAY_SHIFT
    hf = last(w3f_ref) * window
    hb = last(w3b_ref) * window
    ss = (jnp.sum(hf * hf, axis=0, keepdims=True)
          + jnp.sum(hb * hb, axis=0, keepdims=True))
    r = lax.rsqrt(ss + EPS)
    hf_ref[...] = hf * r
    hb_ref[...] = hb * r

    fwd = fwd_ref[...]
    freq = lax.broadcasted_iota(jnp.int32, (p, hf_ref.shape[1]), 0)
    sign = jnp.where((freq & 1) == 0, 1.0, -1.0)

    def block_spectra(h_ref):
        out = []
        for m in range(nb):
            blk = h_ref[m * p:(m + 1) * p, :]
            s = jnp.dot(fwd, blk.astype(BF16), preferred_element_type=F32)
            out.append((s[:p], s[p:], blk[0:1, :]))
        return out

    sf = block_spectra(hf_ref)
    sb = block_spectra(hb_ref)
    mid = N_OFFSETS_HALF
    kr_ref[mid] = sf[0][0] + sb[0][0]
    ki_ref[mid] = sf[0][1] - sb[0][1]
    for d in range(1, nb):
        kr_ref[mid + d] = sf[d][0] - sign * sf[d - 1][1]
        ki_ref[mid + d] = sf[d][1] + sign * (sf[d - 1][0] - sf[d - 1][2])
        kr_ref[mid - d] = sb[d][0] - sign * sb[d - 1][1]
        ki_ref[mid - d] = -(sb[d][1] + sign * (sb[d - 1][0] - sb[d - 1][2]))


def _filter_spectra(seq_len, w1, b1, f1, w2, b2, f2, w3, fwd):
    t = jnp.linspace(0.0, 1.0, seq_len, dtype=F32)[:, None]
    bands = (FILTER_EMB - 1) // 2
    w = 2.0 * math.pi * jnp.arange(seq_len, dtype=F32)[:, None] / seq_len
    fr = jnp.linspace(1e-4, bands - 1, bands, dtype=F32)[None, :]
    feats = jnp.concatenate([t, jnp.cos(fr * w), -jnp.sin(fr * w)], axis=-1)
    feats_t = jnp.pad(feats, ((0, 0), (0, LANES - FILTER_EMB))).T
    col = lambda a: a[:, None]
    w1t = jnp.pad(w1, ((0, LANES - FILTER_EMB), (0, 0))).T
    w2t = w2.T
    max_decay = math.log(DECAY_TARGET) / FAST_DECAY_PCT
    min_decay = math.log(DECAY_TARGET) / SLOW_DECAY_PCT
    deltas = jnp.abs(jnp.linspace(min_decay, max_decay, B_WIDTH, dtype=F32))[None, :]

    tc = CONV_CH_TILE
    n_ct = B_WIDTH // tc
    n_off = 2 * N_OFFSETS_HALF + 1
    fixed = lambda o, c: (0, 0)
    out_sds = jax.ShapeDtypeStruct((HYENA_ORDER, n_off, CONV_BLOCK, B_WIDTH), F32)
    out_spec = pl.BlockSpec((None, n_off, CONV_BLOCK, tc), lambda o, c: (o, 0, 0, c))
    return pl.pallas_call(
        _filter_spec_kernel,
        grid=(HYENA_ORDER, n_ct),
        in_specs=[
            _resident((LANES, seq_len), fixed),
            _resident((FILTER_HIDDEN, LANES), fixed),
            _resident((FILTER_HIDDEN, 1), fixed),
            _resident((FILTER_HIDDEN, 1), fixed),
            _resident((FILTER_HIDDEN, FILTER_HIDDEN), fixed),
            _resident((FILTER_HIDDEN, 1), fixed),
            _resident((FILTER_HIDDEN, 1), fixed),
            pl.BlockSpec((FILTER_HIDDEN, tc), lambda o, c: (0, (2 * o) * n_ct + c)),
            pl.BlockSpec((FILTER_HIDDEN, tc), lambda o, c: (0, (2 * o + 1) * n_ct + c)),
            _resident((seq_len, 1), fixed),
            pl.BlockSpec((1, tc), lambda o, c: (0, c)),
            _resident((2 * CONV_BLOCK, CONV_BLOCK), fixed),
        ],
        out_specs=[out_spec, out_spec],
        out_shape=[out_sds, out_sds],
        scratch_shapes=[
            pltpu.VMEM((FILTER_HIDDEN, seq_len), F32),
            pltpu.VMEM((seq_len, tc), F32),
            pltpu.VMEM((seq_len, tc), F32),
        ],
        compiler_params=_compiler_params(("arbitrary", "arbitrary")),
        name="filter_spec",
    )(feats_t, w1t, col(b1), col(f1), w2t, col(b2), col(f2), w3, w3, t, deltas, fwd)


def _short_conv(src_ref, pad_ref, w_ref, b_ref, dst_ref):
    n = src_ref.shape[0]
    q = n // ROW_PHASES
    band = jnp.zeros((SUBLANES, LANES), F32)
    for s in range(pad_ref.shape[0]):
        lanes = slice(s * LANES, (s + 1) * LANES)
        pad_ref[s, 0:SUBLANES, :] = band
        pad_ref[s, n + SUBLANES:n + 2 * SUBLANES, :] = band
        pad_ref[s, SUBLANES:n + SUBLANES, :] = src_ref[:, lanes].astype(F32)
        w0 = w_ref[0:1, lanes]
        w1 = w_ref[1:2, lanes]
        w2 = w_ref[2:3, lanes]
        bias = b_ref[:, lanes]
        for k in range(ROW_PHASES):
            base = SUBLANES + k
            prev = pad_ref[s, pl.ds(base - 1, q, stride=ROW_PHASES), :]
            cur = pad_ref[s, pl.ds(base, q, stride=ROW_PHASES), :]
            nxt = pad_ref[s, pl.ds(base + 1, q, stride=ROW_PHASES), :]
            dst_ref[s, pl.ds(k, q, stride=ROW_PHASES), :] = (
                prev * w0 + cur * w1 + nxt * w2 + bias)


def _hyena_conv_kernel(z_ref, g_ref, cwz_ref, cbz_ref, cwg_ref, cbg_ref, skip_ref,
                       fwd_ref, inv_ref, kr_ref, ki_ref, o_ref, zf_ref, gf_ref, zs_ref,
                       pad_ref, *, conv_z):
    p = CONV_BLOCK
    nb = z_ref.shape[0] // p
    n_slabs = pad_ref.shape[0]
    if conv_z:
        _short_conv(z_ref, pad_ref, cwz_ref, cbz_ref, zf_ref)
    _short_conv(g_ref, pad_ref, cwg_ref, cbg_ref, gf_ref)

    fwd = fwd_ref[...]
    for j in range(nb):
        rows = slice(j * p, (j + 1) * p)
        if conv_z:
            zb = jnp.concatenate([zf_ref[s, rows, :] for s in range(n_slabs)],
                                 axis=1).astype(BF16)
        else:
            zb = z_ref[rows, :]
        zs_ref[j] = jnp.dot(fwd, zb, preferred_element_type=F32)

    inv = inv_ref[...]
    for i in range(nb):
        yr = None
        yi = None
        for j in range(nb):
            d = i - j + N_OFFSETS_HALF
            zr = zs_ref[j, :p, :]
            zi = zs_ref[j, p:, :]
            kr = kr_ref[d]
            ki = ki_ref[d]
            tr = zr * kr - zi * ki
            ti = zr * ki + zi * kr
            yr = tr if yr is None else yr + tr
            yi = ti if yi is None else yi + ti
        spec = jnp.concatenate([yr, yi], axis=0).astype(BF16)
        y = jnp.dot(inv, spec, preferred_element_type=F32)
        rows = slice(i * p, (i + 1) * p)
        for s in range(n_slabs):
            lanes = slice(s * LANES, (s + 1) * LANES)
            z_blk = zf_ref[s, rows, :] if conv_z else z_ref[rows, lanes].astype(F32)
            o_ref[rows, lanes] = (gf_ref[s, rows, :]
                                  * (y[:, lanes] + z_blk * skip_ref[:, lanes])
                                  ).astype(o_ref.dtype)


def _hyena_conv(z_arr, z_col, g_arr, g_col, conv_w, conv_b, skip, fwd, inv, kr, ki, order,
                conv_z):
    bsz, seq_len, _ = z_arr.shape
    tc = CONV_CH_TILE
    n_ct = B_WIDTH // tc
    n_off = 2 * N_OFFSETS_HALF + 1
    nb = seq_len // CONV_BLOCK
    n_slabs = tc // LANES
    cwz_col = z_col if conv_z else 0
    kernel = functools.partial(_hyena_conv_kernel, conv_z=conv_z)
    return pl.pallas_call(
        kernel,
        grid=(n_ct, bsz),
        in_specs=[
            pl.BlockSpec((None, seq_len, tc), lambda c, b: (b, 0, z_col + c)),
            pl.BlockSpec((None, seq_len, tc), lambda c, b: (b, 0, g_col + c)),
            pl.BlockSpec((SHORT_CONV, tc), lambda c, b: (0, cwz_col + c)),
            pl.BlockSpec((1, tc), lambda c, b: (0, cwz_col + c)),
            pl.BlockSpec((SHORT_CONV, tc), lambda c, b: (0, g_col + c)),
            pl.BlockSpec((1, tc), lambda c, b: (0, g_col + c)),
            pl.BlockSpec((None, 1, tc), lambda c, b: (order, 0, c)),
            pl.BlockSpec((2 * CONV_BLOCK, CONV_BLOCK), lambda c, b: (0, 0)),
            pl.BlockSpec((CONV_BLOCK, 2 * CONV_BLOCK), lambda c, b: (0, 0)),
            pl.BlockSpec((None, n_off, CONV_BLOCK, tc), lambda c, b: (order, 0, 0, c)),
            pl.BlockSpec((None, n_off, CONV_BLOCK, tc), lambda c, b: (order, 0, 0, c)),
        ],
        out_specs=pl.BlockSpec((None, seq_len, tc), lambda c, b: (b, 0, c)),
        out_shape=jax.ShapeDtypeStruct((bsz, seq_len, B_WIDTH), BF16),
        scratch_shapes=[
            pltpu.VMEM((n_slabs, seq_len, LANES), F32),
            pltpu.VMEM((n_slabs, seq_len, LANES), F32),
            pltpu.VMEM((nb, 2 * CONV_BLOCK, tc), F32),
            pltpu.VMEM((n_slabs, seq_len + 2 * SUBLANES, LANES), F32),
        ],
        compiler_params=_compiler_params(("parallel", "parallel")),
        name="hyena_conv%d" % order,
    )(z_arr, g_arr, conv_w, conv_b, conv_w, conv_b, skip, fwd, inv, kr, ki)


def _mix_ffn_kernel(u_ref, v_ref, yb_ref, gate_ref, x_ref, wcat_ref, sbias_ref, woa_ref,
                    wob_ref, wo_ref, gmix_ref, gpre_ref, w1_ref, w2_ref, gpost_ref, o_ref,
                    ya_ref):
    n_pairs = A_WIDTH // LANES
    lane = lax.broadcasted_iota(jnp.int32, (CHUNK, LANES), 1)
    low_half = lane < A_GROUP_DIM
    zero = jnp.zeros((CHUNK, LANES), BF16)
    for sub in _sub_tiles(u_ref.shape[0]):
        for c in range(sub.start, sub.stop, CHUNK):
            rows = slice(c, c + CHUNK)
            parts = []
            for j in range(n_pairs):
                vt = v_ref[rows, j * LANES:(j + 1) * LANES]
                rhs = jnp.concatenate([jnp.where(low_half, vt, zero),
                                       jnp.where(low_half, zero, vt)], axis=0)
                parts.append(jnp.dot(wcat_ref[j], rhs, preferred_element_type=F32))
            s = jnp.concatenate(parts, axis=1) + sbias_ref[...]
            ya_ref[rows, :] = (u_ref[rows, :].astype(F32) * s).astype(BF16)

        ya = jnp.dot(ya_ref[sub, :], woa_ref[...], preferred_element_type=F32)
        yb = jnp.dot(yb_ref[sub, :], wob_ref[...], preferred_element_type=F32)
        ga = gate_ref[sub, :D_MODEL].astype(F32)
        gb = gate_ref[sub, D_MODEL:].astype(F32)
        mix = (ga * ya + gb * yb).astype(BF16)
        m = jnp.dot(mix, wo_ref[...], preferred_element_type=F32)
        h = x_ref[sub, :] + m * _rms_scale(m) * gmix_ref[...]

        hn = (h * _rms_scale(h) * gpre_ref[...]).astype(BF16)
        acc = None
        for k in range(D_FF // FF_CHUNK):
            cols = slice(k * FF_CHUNK, (k + 1) * FF_CHUNK)
            f = jnp.dot(hn, w1_ref[:, cols], preferred_element_type=F32)
            f = jnp.square(jnp.maximum(f, 0.0)).astype(BF16)
            part = jnp.dot(f, w2_ref[cols, :], preferred_element_type=F32)
            acc = part if acc is None else acc + part
        o_ref[sub, :] = h + acc * _rms_scale(acc) * gpost_ref[...]


def _mix_ffn(u, v, yb, gates, x2, wcat, sbias, w_out_a, w_out_b, w_o, g_post_mix, g_pre_ffn,
             w_ff1, w_ff2, g_post_ffn):
    m = x2.shape[0]
    tm = MIX_TILE
    row = lambda i: (i, 0)
    fixed2 = lambda i: (0, 0)
    fixed3 = lambda i: (0, 0, 0)
    gain = _resident((1, D_MODEL), fixed2)
    return pl.pallas_call(
        _mix_ffn_kernel,
        grid=(m // tm,),
        in_specs=[
            pl.BlockSpec((tm, A_WIDTH), row),
            pl.BlockSpec((tm, A_WIDTH), row),
            pl.BlockSpec((tm, B_WIDTH), row),
            pl.BlockSpec((tm, 2 * D_MODEL), row),
            pl.BlockSpec((tm, D_MODEL), row),
            _resident(wcat.shape, fixed3),
            _resident(sbias.shape, fixed2),
            _resident(w_out_a.shape, fixed2),
            _resident(w_out_b.shape, fixed2),
            _resident(w_o.shape, fixed2),
            gain,
            gain,
            _resident(w_ff1.shape, fixed2),
            _resident(w_ff2.shape, fixed2),
            gain,
        ],
        out_specs=pl.BlockSpec((tm, D_MODEL), row),
        out_shape=jax.ShapeDtypeStruct((m, D_MODEL), F32),
        scratch_shapes=[pltpu.VMEM((tm, A_WIDTH), BF16)],
        compiler_params=_compiler_params(("parallel",)),
        name="mix_ffn",
    )(u, v, yb, gates, x2, wcat, sbias, w_out_a, w_out_b, w_o, g_post_mix, g_pre_ffn,
      w_ff1, w_ff2, g_post_ffn)


def _layer(h2, bsz, seq_len, p):
    u, v, pb, gates = _in_proj(h2, p["g_pre_mix"][None, :], p["w_in"].astype(BF16),
                               p["a_v_gain"][None, :])

    fwd, inv = _dft_matrices()
    kr, ki = _filter_spectra(seq_len, p["b_filt_w1"], p["b_filt_b1"], p["b_filt_f1"],
                             p["b_filt_w2"], p["b_filt_b2"], p["b_filt_f2"],
                             p["b_filt_w3"], fwd)

    pb3 = pb.reshape(bsz, seq_len, -1)
    conv_w = p["b_conv_w"]
    conv_b = p["b_conv_b"][None, :]
    skip = p["b_skip"][:, None, :]
    per = B_WIDTH // CONV_CH_TILE
    z1 = _hyena_conv(pb3, 2 * per, pb3, 0, conv_w, conv_b, skip, fwd, inv, kr, ki, 0, True)
    z2 = _hyena_conv(z1, 0, pb3, per, conv_w, conv_b, skip, fwd, inv, kr, ki, 1, False)
    yb = z2.reshape(bsz * seq_len, B_WIDTH)

    w_s = p["a_w_s"]
    n_pairs = A_GROUPS // 2
    wcat = jnp.concatenate([w_s[0::2], w_s[1::2]], axis=2).astype(BF16)
    assert wcat.shape == (n_pairs, CHUNK, 2 * CHUNK)
    sbias = jnp.repeat(p["a_b_s"].T, A_GROUP_DIM, axis=1)

    return _mix_ffn(u, v, yb, gates, h2, wcat, sbias, p["w_out_a"].astype(BF16),
                    p["w_out_b"].astype(BF16), p["w_o"].astype(BF16),
                    p["g_post_mix"][None, :], p["g_pre_ffn"][None, :],
                    p["w_ff1"].astype(BF16), p["w_ff2"].astype(BF16),
                    p["g_post_ffn"][None, :])


def kernel(x, g_pre_mix, w_in, a_v_gain, a_w_s, a_b_s, w_out_a, b_conv_w, b_conv_b,
           b_filt_w1, b_filt_b1, b_filt_f1, b_filt_w2, b_filt_b2, b_filt_f2, b_filt_w3,
           b_skip, w_out_b, w_o, g_post_mix, g_pre_ffn, w_ff1, w_ff2, g_post_ffn):
    params = dict(g_pre_mix=g_pre_mix, w_in=w_in, a_v_gain=a_v_gain, a_w_s=a_w_s, a_b_s=a_b_s,
                  w_out_a=w_out_a, b_conv_w=b_conv_w, b_conv_b=b_conv_b, b_filt_w1=b_filt_w1,
                  b_filt_b1=b_filt_b1, b_filt_f1=b_filt_f1, b_filt_w2=b_filt_w2,
                  b_filt_b2=b_filt_b2, b_filt_f2=b_filt_f2, b_filt_w3=b_filt_w3, b_skip=b_skip,
                  w_out_b=w_out_b, w_o=w_o, g_post_mix=g_post_mix, g_pre_ffn=g_pre_ffn,
                  w_ff1=w_ff1, w_ff2=w_ff2, g_post_ffn=g_post_ffn)
    bsz, seq_len, d = x.shape
    assert d == D_MODEL and seq_len % CONV_BLOCK == 0
    assert seq_len // CONV_BLOCK == N_OFFSETS_HALF + 1
    assert (bsz * seq_len) % TOKEN_TILE == 0 and seq_len % ROW_PHASES == 0
    h2 = x.reshape(bsz * seq_len, d)
    for i in range(g_pre_mix.shape[0]):
        h2 = _layer(h2, bsz, seq_len, {k: v[i] for k, v in params.items()})
    return h2.reshape(bsz, seq_len, d)
```

```python
import functools
import math

import jax
import jax.numpy as jnp
import numpy as np
from jax import lax
from jax.experimental import pallas as pl
from jax.experimental.pallas import tpu as pltpu

F32 = jnp.float32
BF16 = jnp.bfloat16

D_MODEL = 1024
A_WIDTH = 512
A_GROUPS = 8
A_GROUP_DIM = A_WIDTH // A_GROUPS
CHUNK = 128
B_WIDTH = 512
HYENA_ORDER = 2
SHORT_CONV = 3
FILTER_EMB = 33
FILTER_HIDDEN = 64
DECAY_TARGET = 1e-2
FAST_DECAY_PCT = 0.3
SLOW_DECAY_PCT = 1.5
DECAY_SHIFT = 0.05
D_FF = 4 * D_MODEL
EPS = 1e-6

LANES = 128
SUBLANES = 8
CONV_BLOCK = 512
N_OFFSETS_HALF = 3
TOKEN_TILE = 1024
MIX_TILE = 512
SUB_TILE = 512
CONV_CH_TILE = 256
ROW_PHASES = 4
FF_CHUNK = 1024
VMEM_LIMIT = 56 * 1024 * 1024


def _compiler_params(semantics):
    return pltpu.CompilerParams(dimension_semantics=semantics,
                                vmem_limit_bytes=VMEM_LIMIT)


def _resident(block_shape, index_map):
    return pl.BlockSpec(block_shape, index_map, pipeline_mode=pl.Buffered(1))


def _rms_scale(x):
    return lax.rsqrt(jnp.mean(x * x, axis=-1, keepdims=True) + EPS)


def _gelu_tanh(x):
    c = math.sqrt(2.0 / math.pi)
    return x * (0.5 * (1.0 + jnp.tanh(c * (x + 0.044715 * (x * x * x)))))


def _cast_specs(weights, n_steps, step_of):
    in_specs, out_specs, out_shapes = [], [], []
    for w in weights:
        rows, rem = divmod(w.shape[0], n_steps)
        assert rem == 0 and rows % (2 * SUBLANES) == 0
        spec = pl.BlockSpec((rows, w.shape[1]), lambda *idx: (step_of(*idx), 0))
        in_specs.append(spec)
        out_specs.append(spec)
        out_shapes.append(jax.ShapeDtypeStruct(w.shape, BF16))
    return in_specs, out_specs, out_shapes


def _cast_slices(src_refs, dst_refs):
    for src, dst in zip(src_refs, dst_refs):
        dst[...] = src[...].astype(BF16)


def _sub_tiles(n_rows):
    return [slice(s, s + SUB_TILE) for s in range(0, n_rows, SUB_TILE)]


def _in_proj_kernel(x_ref, g_ref, w_ref, vg_ref, woa32_ref, wob32_ref, wo32_ref,
                    u_ref, v_ref, pb_ref, gate_ref, woa_ref, wob_ref, wo_ref):
    _cast_slices((woa32_ref, wob32_ref, wo32_ref), (woa_ref, wob_ref, wo_ref))
    split_a = 2 * A_WIDTH
    split_b = split_a + (HYENA_ORDER + 1) * B_WIDTH
    for rows in _sub_tiles(x_ref.shape[0]):
        x = x_ref[rows, :]
        xn = (x * _rms_scale(x) * g_ref[...]).astype(BF16)

        pa = _gelu_tanh(jnp.dot(xn, w_ref[:, :split_a], preferred_element_type=F32))
        u_ref[rows, :] = pa[:, :A_WIDTH].astype(BF16)
        v = pa[:, A_WIDTH:]
        mu = jnp.mean(v, axis=-1, keepdims=True)
        vc = v - mu
        var = jnp.mean(vc * vc, axis=-1, keepdims=True)
        v_ref[rows, :] = (vc * lax.rsqrt(var + EPS) * vg_ref[...]).astype(BF16)

        pb_ref[rows, :] = jnp.dot(xn, w_ref[:, split_a:split_b],
                                  preferred_element_type=F32).astype(BF16)
        pg = jnp.dot(xn, w_ref[:, split_b:], preferred_element_type=F32)
        gate_ref[rows, :] = jax.nn.sigmoid(pg).astype(BF16)


def _in_proj(x2, g_pre, w_in, v_gain, later_weights):
    m = x2.shape[0]
    in_cols = w_in.shape[1]
    tm = TOKEN_TILE
    pb_cols = (HYENA_ORDER + 1) * B_WIDTH
    gate_cols = 2 * D_MODEL
    row = lambda i: (i, 0)
    fixed = lambda i: (0, 0)
    cast_in, cast_out, cast_shapes = _cast_specs(later_weights, m // tm, lambda i: i)
    outs = pl.pallas_call(
        _in_proj_kernel,
        grid=(m // tm,),
        in_specs=[
            pl.BlockSpec((tm, D_MODEL), row),
            _resident((1, D_MODEL), fixed),
            _resident((D_MODEL, in_cols), fixed),
            _resident((1, A_WIDTH), fixed),
        ] + cast_in,
        out_specs=[
            pl.BlockSpec((tm, A_WIDTH), row),
            pl.BlockSpec((tm, A_WIDTH), row),
            pl.BlockSpec((tm, pb_cols), row),
            pl.BlockSpec((tm, gate_cols), row),
        ] + cast_out,
        out_shape=[
            jax.ShapeDtypeStruct((m, A_WIDTH), BF16),
            jax.ShapeDtypeStruct((m, A_WIDTH), BF16),
            jax.ShapeDtypeStruct((m, pb_cols), BF16),
            jax.ShapeDtypeStruct((m, gate_cols), BF16),
        ] + cast_shapes,
        compiler_params=_compiler_params(("parallel",)),
        name="in_proj",
    )(x2, g_pre, w_in, v_gain, *later_weights)
    return outs[:4], outs[4:]


def _dft_matrices():
    p = CONV_BLOCK
    n = 4 * p
    f = np.arange(p, dtype=np.int64)[:, None]
    t = np.arange(p, dtype=np.int64)[None, :]
    theta = (((2 * f + 1) * t) % n).astype(np.float64) * (2.0 * math.pi / n)
    c, s = np.cos(theta), np.sin(theta)
    fwd = np.concatenate([c, -s], axis=0).astype(np.float32)
    inv = (np.concatenate([c.T, -s.T], axis=1) * (1.0 / p)).astype(np.float32)
    return jnp.asarray(fwd).astype(BF16), jnp.asarray(inv).astype(BF16)


def _filter_spec_kernel(feats_t_ref, w1t_ref, b1_ref, f1_ref, w2t_ref, b2_ref, f2_ref,
                        w3f_ref, w3b_ref, t_ref, deltas_ref, fwd_ref, win32_ref, kr_ref, ki_ref,
                        win_ref, h2t_ref, hf_ref, hb_ref):
    _cast_slices((win32_ref,), (win_ref,))
    p = CONV_BLOCK
    nb = hf_ref.shape[0] // p
    hp = lax.Precision.HIGHEST

    @pl.when((pl.program_id(0) == 0) & (pl.program_id(1) == 0))
    def _():
        a = jnp.sin(f1_ref[...] * (jnp.dot(w1t_ref[...], feats_t_ref[...], precision=hp,
                                           preferred_element_type=F32) + b1_ref[...]))
        h2t_ref[...] = jnp.sin(f2_ref[...] * (jnp.dot(w2t_ref[...], a, precision=hp,
                                                      preferred_element_type=F32)
                                              + b2_ref[...]))

    h2t = h2t_ref[...]
    h_hi = h2t.astype(BF16)
    h_lo = (h2t - h_hi.astype(F32)).astype(BF16)
    h_stack = jnp.concatenate([h_hi, h_hi, h_lo], axis=0)
    over_hidden = (((0,), (0,)), ((), ()))

    def last(w_ref):
        w = w_ref[...]
        w_hi = w.astype(BF16)
        w_lo = (w - w_hi.astype(F32)).astype(BF16)
        w_stack = jnp.concatenate([w_hi, w_lo, w_hi], axis=0)
        return lax.dot_general(h_stack, w_stack, over_hidden, preferred_element_type=F32)

    window = jnp.exp(-t_ref[...] * deltas_ref[...]) + DECAY_SHIFT
    hf = last(w3f_ref) * window
    hb = last(w3b_ref) * window
    ss = (jnp.sum(hf * hf, axis=0, keepdims=True)
          + jnp.sum(hb * hb, axis=0, keepdims=True))
    r = lax.rsqrt(ss + EPS)
    hf_ref[...] = hf * r
    hb_ref[...] = hb * r

    fwd = fwd_ref[...]
    freq = lax.broadcasted_iota(jnp.int32, (p, hf_ref.shape[1]), 0)
    sign = jnp.where((freq & 1) == 0, 1.0, -1.0)

    def block_spectra(h_ref):
        out = []
        for m in range(nb):
            blk = h_ref[m * p:(m + 1) * p, :]
            s = jnp.dot(fwd, blk.astype(BF16), preferred_element_type=F32)
            out.append((s[:p], s[p:], blk[0:1, :]))
        return out

    sf = block_spectra(hf_ref)
    sb = block_spectra(hb_ref)
    mid = N_OFFSETS_HALF
    kr_ref[mid] = sf[0][0] + sb[0][0]
    ki_ref[mid] = sf[0][1] - sb[0][1]
    for d in range(1, nb):
        kr_ref[mid + d] = sf[d][0] - sign * sf[d - 1][1]
        ki_ref[mid + d] = sf[d][1] + sign * (sf[d - 1][0] - sf[d - 1][2])
        kr_ref[mid - d] = sb[d][0] - sign * sb[d - 1][1]
        ki_ref[mid - d] = -(sb[d][1] + sign * (sb[d - 1][0] - sb[d - 1][2]))


def _filter_spectra(seq_len, w1, b1, f1, w2, b2, f2, w3, fwd, w_in):
    t = jnp.linspace(0.0, 1.0, seq_len, dtype=F32)[:, None]
    bands = (FILTER_EMB - 1) // 2
    w = 2.0 * math.pi * jnp.arange(seq_len, dtype=F32)[:, None] / seq_len
    fr = jnp.linspace(1e-4, bands - 1, bands, dtype=F32)[None, :]
    feats = jnp.concatenate([t, jnp.cos(fr * w), -jnp.sin(fr * w)], axis=-1)
    feats_t = jnp.pad(feats, ((0, 0), (0, LANES - FILTER_EMB))).T
    col = lambda a: a[:, None]
    w1t = jnp.pad(w1, ((0, LANES - FILTER_EMB), (0, 0))).T
    w2t = w2.T
    max_decay = math.log(DECAY_TARGET) / FAST_DECAY_PCT
    min_decay = math.log(DECAY_TARGET) / SLOW_DECAY_PCT
    deltas = jnp.abs(jnp.linspace(min_decay, max_decay, B_WIDTH, dtype=F32))[None, :]

    tc = CONV_CH_TILE
    n_ct = B_WIDTH // tc
    n_off = 2 * N_OFFSETS_HALF + 1
    fixed = lambda o, c: (0, 0)
    out_sds = jax.ShapeDtypeStruct((HYENA_ORDER, n_off, CONV_BLOCK, B_WIDTH), F32)
    out_spec = pl.BlockSpec((None, n_off, CONV_BLOCK, tc), lambda o, c: (o, 0, 0, c))
    cast_in, cast_out, cast_shapes = _cast_specs([w_in], HYENA_ORDER * n_ct,
                                                 lambda o, c: o * n_ct + c)
    return pl.pallas_call(
        _filter_spec_kernel,
        grid=(HYENA_ORDER, n_ct),
        in_specs=[
            _resident((LANES, seq_len), fixed),
            _resident((FILTER_HIDDEN, LANES), fixed),
            _resident((FILTER_HIDDEN, 1), fixed),
            _resident((FILTER_HIDDEN, 1), fixed),
            _resident((FILTER_HIDDEN, FILTER_HIDDEN), fixed),
            _resident((FILTER_HIDDEN, 1), fixed),
            _resident((FILTER_HIDDEN, 1), fixed),
            pl.BlockSpec((FILTER_HIDDEN, tc), lambda o, c: (0, (2 * o) * n_ct + c)),
            pl.BlockSpec((FILTER_HIDDEN, tc), lambda o, c: (0, (2 * o + 1) * n_ct + c)),
            _resident((seq_len, 1), fixed),
            pl.BlockSpec((1, tc), lambda o, c: (0, c)),
            _resident((2 * CONV_BLOCK, CONV_BLOCK), fixed),
        ] + cast_in,
        out_specs=[out_spec, out_spec] + cast_out,
        out_shape=[out_sds, out_sds] + cast_shapes,
        scratch_shapes=[
            pltpu.VMEM((FILTER_HIDDEN, seq_len), F32),
            pltpu.VMEM((seq_len, tc), F32),
            pltpu.VMEM((seq_len, tc), F32),
        ],
        compiler_params=_compiler_params(("arbitrary", "arbitrary")),
        name="filter_spec",
    )(feats_t, w1t, col(b1), col(f1), w2t, col(b2), col(f2), w3, w3, t, deltas, fwd, w_in)


def _short_conv(src_ref, pad_ref, w_ref, b_ref, dst_ref):
    n = src_ref.shape[0]
    q = n // ROW_PHASES
    band = jnp.zeros((SUBLANES, LANES), F32)
    for s in range(pad_ref.shape[0]):
        lanes = slice(s * LANES, (s + 1) * LANES)
        pad_ref[s, 0:SUBLANES, :] = band
        pad_ref[s, n + SUBLANES:n + 2 * SUBLANES, :] = band
        pad_ref[s, SUBLANES:n + SUBLANES, :] = src_ref[:, lanes].astype(F32)
        w0 = w_ref[0:1, lanes]
        w1 = w_ref[1:2, lanes]
        w2 = w_ref[2:3, lanes]
        bias = b_ref[:, lanes]
        for k in range(ROW_PHASES):
            base = SUBLANES + k
            prev = pad_ref[s, pl.ds(base - 1, q, stride=ROW_PHASES), :]
            cur = pad_ref[s, pl.ds(base, q, stride=ROW_PHASES), :]
            nxt = pad_ref[s, pl.ds(base + 1, q, stride=ROW_PHASES), :]
            dst_ref[s, pl.ds(k, q, stride=ROW_PHASES), :] = (
                prev * w0 + cur * w1 + nxt * w2 + bias)


def _hyena_conv_kernel(z_ref, g_ref, cwz_ref, cbz_ref, cwg_ref, cbg_ref, skip_ref,
                       fwd_ref, inv_ref, kr_ref, ki_ref, w32_ref, o_ref, w16_ref, zf_ref,
                       gf_ref, zs_ref, pad_ref, *, conv_z):
    _cast_slices((w32_ref,), (w16_ref,))
    p = CONV_BLOCK
    nb = z_ref.shape[0] // p
    n_slabs = pad_ref.shape[0]
    if conv_z:
        _short_conv(z_ref, pad_ref, cwz_ref, cbz_ref, zf_ref)
    _short_conv(g_ref, pad_ref, cwg_ref, cbg_ref, gf_ref)

    fwd = fwd_ref[...]
    for j in range(nb):
        rows = slice(j * p, (j + 1) * p)
        if conv_z:
            zb = jnp.concatenate([zf_ref[s, rows, :] for s in range(n_slabs)],
                                 axis=1).astype(BF16)
        else:
            zb = z_ref[rows, :]
        zs_ref[j] = jnp.dot(fwd, zb, preferred_element_type=F32)

    inv = inv_ref[...]
    for i in range(nb):
        yr = None
        yi = None
        for j in range(nb):
            d = i - j + N_OFFSETS_HALF
            zr = zs_ref[j, :p, :]
            zi = zs_ref[j, p:, :]
            kr = kr_ref[d]
            ki = ki_ref[d]
            tr = zr * kr - zi * ki
            ti = zr * ki + zi * kr
            yr = tr if yr is None else yr + tr
            yi = ti if yi is None else yi + ti
        spec = jnp.concatenate([yr, yi], axis=0).astype(BF16)
        y = jnp.dot(inv, spec, preferred_element_type=F32)
        rows = slice(i * p, (i + 1) * p)
        for s in range(n_slabs):
            lanes = slice(s * LANES, (s + 1) * LANES)
            z_blk = zf_ref[s, rows, :] if conv_z else z_ref[rows, lanes].astype(F32)
            o_ref[rows, lanes] = (gf_ref[s, rows, :]
                                  * (y[:, lanes] + z_blk * skip_ref[:, lanes])
                                  ).astype(o_ref.dtype)


def _hyena_conv(z_arr, z_col, g_arr, g_col, conv_w, conv_b, skip, fwd, inv, kr, ki, order,
                conv_z, later_weight):
    bsz, seq_len, _ = z_arr.shape
    tc = CONV_CH_TILE
    n_ct = B_WIDTH // tc
    n_off = 2 * N_OFFSETS_HALF + 1
    nb = seq_len // CONV_BLOCK
    n_slabs = tc // LANES
    cwz_col = z_col if conv_z else 0
    kernel = functools.partial(_hyena_conv_kernel, conv_z=conv_z)
    cast_in, cast_out, cast_shapes = _cast_specs([later_weight], n_ct * bsz,
                                                 lambda c, b: c * bsz + b)
    return pl.pallas_call(
        kernel,
        grid=(n_ct, bsz),
        in_specs=[
            pl.BlockSpec((None, seq_len, tc), lambda c, b: (b, 0, z_col + c)),
            pl.BlockSpec((None, seq_len, tc), lambda c, b: (b, 0, g_col + c)),
            pl.BlockSpec((SHORT_CONV, tc), lambda c, b: (0, cwz_col + c)),
            pl.BlockSpec((1, tc), lambda c, b: (0, cwz_col + c)),
            pl.BlockSpec((SHORT_CONV, tc), lambda c, b: (0, g_col + c)),
            pl.BlockSpec((1, tc), lambda c, b: (0, g_col + c)),
            pl.BlockSpec((None, 1, tc), lambda c, b: (order, 0, c)),
            pl.BlockSpec((2 * CONV_BLOCK, CONV_BLOCK), lambda c, b: (0, 0)),
            pl.BlockSpec((CONV_BLOCK, 2 * CONV_BLOCK), lambda c, b: (0, 0)),
            pl.BlockSpec((None, n_off, CONV_BLOCK, tc), lambda c, b: (order, 0, 0, c)),
            pl.BlockSpec((None, n_off, CONV_BLOCK, tc), lambda c, b: (order, 0, 0, c)),
        ] + cast_in,
        out_specs=[pl.BlockSpec((None, seq_len, tc), lambda c, b: (b, 0, c))] + cast_out,
        out_shape=[jax.ShapeDtypeStruct((bsz, seq_len, B_WIDTH), BF16)] + cast_shapes,
        scratch_shapes=[
            pltpu.VMEM((n_slabs, seq_len, LANES), F32),
            pltpu.VMEM((n_slabs, seq_len, LANES), F32),
            pltpu.VMEM((nb, 2 * CONV_BLOCK, tc), F32),
            pltpu.VMEM((n_slabs, seq_len + 2 * SUBLANES, LANES), F32),
        ],
        compiler_params=_compiler_params(("parallel", "parallel")),
        name="hyena_conv%d" % order,
    )(z_arr, g_arr, conv_w, conv_b, conv_w, conv_b, skip, fwd, inv, kr, ki, later_weight)


def _mix_ffn_kernel(u_ref, v_ref, yb_ref, gate_ref, x_ref, wcat_ref, sbias_ref, woa_ref,
                    wob_ref, wo_ref, gmix_ref, gpre_ref, w1_ref, w2_ref, gpost_ref, o_ref,
                    ya_ref):
    n_pairs = A_WIDTH // LANES
    lane = lax.broadcasted_iota(jnp.int32, (CHUNK, LANES), 1)
    low_half = lane < A_GROUP_DIM
    zero = jnp.zeros((CHUNK, LANES), BF16)
    for sub in _sub_tiles(u_ref.shape[0]):
        for c in range(sub.start, sub.stop, CHUNK):
            rows = slice(c, c + CHUNK)
            parts = []
            for j in range(n_pairs):
                vt = v_ref[rows, j * LANES:(j + 1) * LANES]
                rhs = jnp.concatenate([jnp.where(low_half, vt, zero),
                                       jnp.where(low_half, zero, vt)], axis=0)
                parts.append(jnp.dot(wcat_ref[j], rhs, preferred_element_type=F32))
            s = jnp.concatenate(parts, axis=1) + sbias_ref[...]
            ya_ref[rows, :] = (u_ref[rows, :].astype(F32) * s).astype(BF16)

        ya = jnp.dot(ya_ref[sub, :], woa_ref[...], preferred_element_type=F32)
        yb = jnp.dot(yb_ref[sub, :], wob_ref[...], preferred_element_type=F32)
        ga = gate_ref[sub, :D_MODEL].astype(F32)
        gb = gate_ref[sub, D_MODEL:].astype(F32)
        mix = (ga * ya + gb * yb).astype(BF16)
        m = jnp.dot(mix, wo_ref[...], preferred_element_type=F32)
        h = x_ref[sub, :] + m * _rms_scale(m) * gmix_ref[...]

        hn = (h * _rms_scale(h) * gpre_ref[...]).astype(BF16)
        acc = None
        for k in range(D_FF // FF_CHUNK):
            cols = slice(k * FF_CHUNK, (k + 1) * FF_CHUNK)
            f = jnp.dot(hn, w1_ref[:, cols], preferred_element_type=F32)
            f = jnp.square(jnp.maximum(f, 0.0)).astype(BF16)
            part = jnp.dot(f, w2_ref[cols, :], preferred_element_type=F32)
            acc = part if acc is None else acc + part
        o_ref[sub, :] = h + acc * _rms_scale(acc) * gpost_ref[...]


def _mix_ffn(u, v, yb, gates, x2, wcat, sbias, w_out_a, w_out_b, w_o, g_post_mix, g_pre_ffn,
             w_ff1, w_ff2, g_post_ffn):
    m = x2.shape[0]
    tm = MIX_TILE
    row = lambda i: (i, 0)
    fixed2 = lambda i: (0, 0)
    fixed3 = lambda i: (0, 0, 0)
    gain = _resident((1, D_MODEL), fixed2)
    return pl.pallas_call(
        _mix_ffn_kernel,
        grid=(m // tm,),
        in_specs=[
            pl.BlockSpec((tm, A_WIDTH), row),
            pl.BlockSpec((tm, A_WIDTH), row),
            pl.BlockSpec((tm, B_WIDTH), row),
            pl.BlockSpec((tm, 2 * D_MODEL), row),
            pl.BlockSpec((tm, D_MODEL), row),
            _resident(wcat.shape, fixed3),
            _resident(sbias.shape, fixed2),
            _resident(w_out_a.shape, fixed2),
            _resident(w_out_b.shape, fixed2),
            _resident(w_o.shape, fixed2),
            gain,
            gain,
            _resident(w_ff1.shape, fixed2),
            _resident(w_ff2.shape, fixed2),
            gain,
        ],
        out_specs=pl.BlockSpec((tm, D_MODEL), row),
        out_shape=jax.ShapeDtypeStruct((m, D_MODEL), F32),
        scratch_shapes=[pltpu.VMEM((tm, A_WIDTH), BF16)],
        compiler_params=_compiler_params(("parallel",)),
        name="mix_ffn",
    )(u, v, yb, gates, x2, wcat, sbias, w_out_a, w_out_b, w_o, g_post_mix, g_pre_ffn,
      w_ff1, w_ff2, g_post_ffn)


def _layer(h2, bsz, seq_len, p):
    fwd, inv = _dft_matrices()
    kr, ki, w_in = _filter_spectra(seq_len, p["b_filt_w1"], p["b_filt_b1"], p["b_filt_f1"],
                                   p["b_filt_w2"], p["b_filt_b2"], p["b_filt_f2"],
                                   p["b_filt_w3"], fwd, p["w_in"])
    (u, v, pb, gates), (w_out_a, w_out_b, w_o) = _in_proj(
        h2, p["g_pre_mix"][None, :], w_in, p["a_v_gain"][None, :],
        [p["w_out_a"], p["w_out_b"], p["w_o"]])

    pb3 = pb.reshape(bsz, seq_len, -1)
    conv_w = p["b_conv_w"]
    conv_b = p["b_conv_b"][None, :]
    skip = p["b_skip"][:, None, :]
    per = B_WIDTH // CONV_CH_TILE
    z1, w_ff1 = _hyena_conv(pb3, 2 * per, pb3, 0, conv_w, conv_b, skip, fwd, inv, kr, ki, 0,
                            True, p["w_ff1"])
    z2, w_ff2 = _hyena_conv(z1, 0, pb3, per, conv_w, conv_b, skip, fwd, inv, kr, ki, 1,
                            False, p["w_ff2"])
    yb = z2.reshape(bsz * seq_len, B_WIDTH)

    w_s = p["a_w_s"]
    n_pairs = A_GROUPS // 2
    wcat = jnp.concatenate([w_s[0::2], w_s[1::2]], axis=2).astype(BF16)
    assert wcat.shape == (n_pairs, CHUNK, 2 * CHUNK)
    sbias = jnp.repeat(p["a_b_s"].T, A_GROUP_DIM, axis=1)

    return _mix_ffn(u, v, yb, gates, h2, wcat, sbias, w_out_a, w_out_b, w_o,
                    p["g_post_mix"][None, :], p["g_pre_ffn"][None, :], w_ff1, w_ff2,
                    p["g_post_ffn"][None, :])


def kernel(x, g_pre_mix, w_in, a_v_gain, a_w_s, a_b_s, w_out_a, b_conv_w, b_conv_b,
           b_filt_w1, b_filt_b1, b_filt_f1, b_filt_w2, b_filt_b2, b_filt_f2, b_filt_w3,
           b_skip, w_out_b, w_o, g_post_mix, g_pre_ffn, w_ff1, w_ff2, g_post_ffn):
    params = dict(g_pre_mix=g_pre_mix, w_in=w_in, a_v_gain=a_v_gain, a_w_s=a_w_s, a_b_s=a_b_s,
                  w_out_a=w_out_a, b_conv_w=b_conv_w, b_conv_b=b_conv_b, b_filt_w1=b_filt_w1,
                  b_filt_b1=b_filt_b1, b_filt_f1=b_filt_f1, b_filt_w2=b_filt_w2,
                  b_filt_b2=b_filt_b2, b_filt_f2=b_filt_f2, b_filt_w3=b_filt_w3, b_skip=b_skip,
                  w_out_b=w_out_b, w_o=w_o, g_post_mix=g_post_mix, g_pre_ffn=g_pre_ffn,
                  w_ff1=w_ff1, w_ff2=w_ff2, g_post_ffn=g_post_ffn)
    bsz, seq_len, d = x.shape
    assert d == D_MODEL and seq_len % CONV_BLOCK == 0
    assert seq_len // CONV_BLOCK == N_OFFSETS_HALF + 1
    assert (bsz * seq_len) % TOKEN_TILE == 0 and seq_len % ROW_PHASES == 0
    h2 = x.reshape(bsz * seq_len, d)
    for i in range(g_pre_mix.shape[0]):
        h2 = _layer(h2, bsz, seq_len, {k: v[i] for k, v in params.items()})
    return h2.reshape(bsz, seq_len, d)
```

```python
import functools
import math

import jax
import jax.numpy as jnp
import numpy as np
from jax import lax
from jax.experimental import pallas as pl
from jax.experimental.pallas import tpu as pltpu

F32 = jnp.float32
BF16 = jnp.bfloat16

D_MODEL = 1024
A_WIDTH = 512
A_GROUPS = 8
A_GROUP_DIM = A_WIDTH // A_GROUPS
CHUNK = 128
B_WIDTH = 512
HYENA_ORDER = 2
SHORT_CONV = 3
FILTER_EMB = 33
FILTER_HIDDEN = 64
DECAY_TARGET = 1e-2
FAST_DECAY_PCT = 0.3
SLOW_DECAY_PCT = 1.5
DECAY_SHIFT = 0.05
D_FF = 4 * D_MODEL
EPS = 1e-6

LANES = 128
SUBLANES = 8
CONV_BLOCK = 512
N_BLOCKS = 4
N_COEFS = 9
TOKEN_TILE = 1024
MIX_TILE = 512
MIX_SUB_TILE = 512
SUB_TILE = 512
CONV_CH_TILE = 256
CONV_SEQS = 1
ROW_PHASES = 4
SPEC_CHUNK = 128
SPEC_ROWS = 16
FF_CHUNK = 1024
VMEM_LIMIT = 56 * 1024 * 1024


def _compiler_params(semantics, flags=None):
    return pltpu.CompilerParams(dimension_semantics=semantics,
                                vmem_limit_bytes=VMEM_LIMIT, flags=flags)


def _resident(block_shape, index_map):
    return pl.BlockSpec(block_shape, index_map, pipeline_mode=pl.Buffered(1))


def _rms_scale(x):
    return lax.rsqrt(jnp.mean(x * x, axis=-1, keepdims=True) + EPS)


def _gelu_tanh(x):
    c = math.sqrt(2.0 / math.pi)
    return x * (0.5 * (1.0 + jnp.tanh(c * (x + 0.044715 * (x * x * x)))))


def _cast_specs(weights, n_steps, step_of):
    in_specs, out_specs, out_shapes = [], [], []
    for w in weights:
        rows, rem = divmod(w.shape[0], n_steps)
        assert rem == 0 and rows % (2 * SUBLANES) == 0
        spec = pl.BlockSpec((rows, w.shape[1]), lambda *idx: (step_of(*idx), 0))
        in_specs.append(spec)
        out_specs.append(spec)
        out_shapes.append(jax.ShapeDtypeStruct(w.shape, BF16))
    return in_specs, out_specs, out_shapes


def _cast_slices(src_refs, dst_refs):
    for src, dst in zip(src_refs, dst_refs):
        dst[...] = src[...].astype(BF16)


def _sub_tiles(n_rows, sub=SUB_TILE):
    return [slice(s, s + sub) for s in range(0, n_rows, sub)]


def _in_proj_kernel(x_ref, g_ref, w_ref, vg_ref, woa32_ref, wob32_ref, wo32_ref,
                    u_ref, v_ref, pb_ref, gate_ref, woa_ref, wob_ref, wo_ref):
    _cast_slices((woa32_ref, wob32_ref, wo32_ref), (woa_ref, wob_ref, wo_ref))
    split_a = 2 * A_WIDTH
    split_b = split_a + (HYENA_ORDER + 1) * B_WIDTH
    for rows in _sub_tiles(x_ref.shape[0]):
        x = x_ref[rows, :]
        xn = (x * _rms_scale(x) * g_ref[...]).astype(BF16)

        pg = jnp.dot(xn, w_ref[:, split_b:], preferred_element_type=F32)
        gate_ref[rows, :] = jax.nn.sigmoid(pg).astype(BF16)

        pa = _gelu_tanh(jnp.dot(xn, w_ref[:, :split_a], preferred_element_type=F32))
        u_ref[rows, :] = pa[:, :A_WIDTH].astype(BF16)
        v = pa[:, A_WIDTH:]
        mu = jnp.mean(v, axis=-1, keepdims=True)
        vc = v - mu
        var = jnp.mean(vc * vc, axis=-1, keepdims=True)
        v_ref[rows, :] = (vc * lax.rsqrt(var + EPS) * vg_ref[...]).astype(BF16)

        pb = jnp.dot(xn, w_ref[:, split_a:split_b], preferred_element_type=F32)
        for k in range(pb_ref.shape[0]):
            pb_ref[k, rows, :] = pb[:, k * CONV_CH_TILE:(k + 1) * CONV_CH_TILE].astype(BF16)


def _in_proj(x2, g_pre, w_in, v_gain, later_weights):
    m = x2.shape[0]
    in_cols = w_in.shape[1]
    tm = TOKEN_TILE
    pb_tiles = (HYENA_ORDER + 1) * B_WIDTH // CONV_CH_TILE
    gate_cols = 2 * D_MODEL
    row = lambda i: (i, 0)
    fixed = lambda i: (0, 0)
    cast_in, cast_out, cast_shapes = _cast_specs(later_weights, m // tm, lambda i: i)
    outs = pl.pallas_call(
        _in_proj_kernel,
        grid=(m // tm,),
        in_specs=[
            pl.BlockSpec((tm, D_MODEL), row),
            _resident((1, D_MODEL), fixed),
            _resident((D_MODEL, in_cols), fixed),
            _resident((1, A_WIDTH), fixed),
        ] + cast_in,
        out_specs=[
            pl.BlockSpec((tm, A_WIDTH), row),
            pl.BlockSpec((tm, A_WIDTH), row),
            pl.BlockSpec((pb_tiles, tm, CONV_CH_TILE), lambda i: (0, i, 0)),
            pl.BlockSpec((tm, gate_cols), row),
        ] + cast_out,
        out_shape=[
            jax.ShapeDtypeStruct((m, A_WIDTH), BF16),
            jax.ShapeDtypeStruct((m, A_WIDTH), BF16),
            jax.ShapeDtypeStruct((pb_tiles, m, CONV_CH_TILE), BF16),
            jax.ShapeDtypeStruct((m, gate_cols), BF16),
        ] + cast_shapes,
        compiler_params=_compiler_params(("parallel",)),
        name="in_proj",
    )(x2, g_pre, w_in, v_gain, *later_weights)
    return outs[:4], outs[4:]


def _dft_matrices():
    p = CONV_BLOCK
    n = 4 * p
    f = np.arange(p, dtype=np.int64)[:, None]
    t = np.arange(p, dtype=np.int64)[None, :]
    theta = (((2 * f + 1) * t) % n).astype(np.float64) * (2.0 * math.pi / n)
    c, s = np.cos(theta), np.sin(theta)
    fwd = np.concatenate([c, -s], axis=0).astype(np.float32)
    chunks = [np.concatenate([c.T[:, k:k + SPEC_CHUNK], -s.T[:, k:k + SPEC_CHUNK]], axis=1)
              for k in range(0, p, SPEC_CHUNK)]
    inv = (np.stack(chunks) * (1.0 / p)).astype(np.float32)
    return jnp.asarray(fwd).astype(BF16), jnp.asarray(inv).astype(BF16)


def _filter_spec_kernel(feats_t_ref, w1_ref, w2_ref, vecs_ref, w3f_ref, w3b_ref, t_ref,
                        deltas_ref, fwd_ref, win32_ref, kr_ref, ki_ref, win_ref, h2t_ref,
                        hf_ref, hb_ref):
    _cast_slices((win32_ref,), (win_ref,))
    p = CONV_BLOCK
    nb = hf_ref.shape[0] // p
    hp = lax.Precision.HIGHEST

    over_rows = (((0,), (0,)), ((), ()))

    @pl.when((pl.program_id(0) == 0) & (pl.program_id(1) == 0))
    def _():
        b1, f1, b2, f2 = (vecs_ref[:, k:k + 1] for k in range(4))
        a = jnp.sin(f1 * (lax.dot_general(w1_ref[...], feats_t_ref[...], over_rows,
                                          precision=hp, preferred_element_type=F32) + b1))
        h2t_ref[...] = jnp.sin(f2 * (lax.dot_general(w2_ref[...], a, over_rows, precision=hp,
                                                     preferred_element_type=F32) + b2))

    h2t = h2t_ref[...]
    h_hi = h2t.astype(BF16)
    h_lo = (h2t - h_hi.astype(F32)).astype(BF16)
    h_stack = jnp.concatenate([h_hi, h_hi, h_lo], axis=0)

    def last(w_ref):
        w = w_ref[...]
        w_hi = w.astype(BF16)
        w_lo = (w - w_hi.astype(F32)).astype(BF16)
        w_stack = jnp.concatenate([w_hi, w_lo, w_hi], axis=0)
        return lax.dot_general(h_stack, w_stack, over_rows, preferred_element_type=F32)

    window = jnp.exp(-t_ref[...] * deltas_ref[...]) + DECAY_SHIFT
    hf = last(w3f_ref) * window
    hb = last(w3b_ref) * window
    ss = (jnp.sum(hf * hf, axis=0, keepdims=True)
          + jnp.sum(hb * hb, axis=0, keepdims=True))
    r = lax.rsqrt(ss + EPS)
    hf_ref[...] = hf * r
    hb_ref[...] = hb * r

    fwd = fwd_ref[...]
    freq = lax.broadcasted_iota(jnp.int32, (p, hf_ref.shape[1]), 0)
    sign = jnp.where((freq & 1) == 0, 1.0, -1.0)

    def block_spectra(h_ref):
        out = []
        for m in range(nb):
            blk = h_ref[m * p:(m + 1) * p, :]
            s = jnp.dot(fwd, blk.astype(BF16), preferred_element_type=F32)
            out.append((s[:p], s[p:], blk[0:1, :]))
        return out

    sf = block_spectra(hf_ref)
    sb = block_spectra(hb_ref)
    k = {0: (sf[0][0] + sb[0][0], sf[0][1] - sb[0][1])}
    for d in range(1, nb):
        k[d] = (sf[d][0] - sign * sf[d - 1][1],
                sf[d][1] + sign * (sf[d - 1][0] - sf[d - 1][2]))
        k[-d] = (sb[d][0] - sign * sb[d - 1][1],
                 -(sb[d][1] + sign * (sb[d - 1][0] - sb[d - 1][2])))

    sub = lambda x, y: (x[0] - y[0], x[1] - y[1])
    blocks = [(k[0], k[-1], k[1]),
              (sub(k[-2], k[0]), sub(k[-3], k[-1]), sub(k[-1], k[1])),
              (sub(k[2], k[0]), sub(k[1], k[-1]), sub(k[3], k[1]))]
    n = 0
    for a, b, c in blocks:
        for coef in (a, sub(b, a), sub(c, a)):
            kr_ref[n] = coef[0]
            ki_ref[n] = coef[1]
            n += 1


def _filter_spectra(seq_len, w1, b1, f1, w2, b2, f2, w3, fwd, w_in):
    t = np.linspace(0.0, 1.0, seq_len)[:, None]
    bands = (FILTER_EMB - 1) // 2
    w = 2.0 * math.pi * np.arange(seq_len)[:, None] / seq_len
    fr = np.linspace(1e-4, bands - 1, bands)[None, :]
    feats = np.concatenate([t, np.cos(fr * w), -np.sin(fr * w)], axis=-1)
    feats_t = np.pad(feats, ((0, 0), (0, LANES - FILTER_EMB))).T.astype(np.float32)
    max_decay = math.log(DECAY_TARGET) / FAST_DECAY_PCT
    min_decay = math.log(DECAY_TARGET) / SLOW_DECAY_PCT
    deltas = np.abs(np.linspace(min_decay, max_decay, B_WIDTH))[None, :].astype(np.float32)
    t = t.astype(np.float32)
    w1p = jnp.pad(w1, ((0, LANES - FILTER_EMB), (0, 0)))
    vecs = jnp.stack([b1, f1, b2, f2], axis=1)

    tc = CONV_CH_TILE
    n_ct = B_WIDTH // tc
    n_off = N_COEFS
    fixed = lambda o, c: (0, 0)
    out_sds = jax.ShapeDtypeStruct((HYENA_ORDER, n_ct, n_off, CONV_BLOCK, tc), F32)
    out_spec = pl.BlockSpec((None, None, n_off, CONV_BLOCK, tc),
                            lambda o, c: (o, c, 0, 0, 0))
    cast_in, cast_out, cast_shapes = _cast_specs([w_in], HYENA_ORDER * n_ct,
                                                 lambda o, c: o * n_ct + c)
    return pl.pallas_call(
        _filter_spec_kernel,
        grid=(HYENA_ORDER, n_ct),
        in_specs=[
            _resident((LANES, seq_len), fixed),
            _resident((LANES, FILTER_HIDDEN), fixed),
            _resident((FILTER_HIDDEN, FILTER_HIDDEN), fixed),
            _resident((FILTER_HIDDEN, 4), fixed),
            pl.BlockSpec((FILTER_HIDDEN, tc), lambda o, c: (0, (2 * o) * n_ct + c)),
            pl.BlockSpec((FILTER_HIDDEN, tc), lambda o, c: (0, (2 * o + 1) * n_ct + c)),
            _resident((seq_len, 1), fixed),
            pl.BlockSpec((1, tc), lambda o, c: (0, c)),
            _resident((2 * CONV_BLOCK, CONV_BLOCK), fixed),
        ] + cast_in,
        out_specs=[out_spec, out_spec] + cast_out,
        out_shape=[out_sds, out_sds] + cast_shapes,
        scratch_shapes=[
            pltpu.VMEM((FILTER_HIDDEN, seq_len), F32),
            pltpu.VMEM((seq_len, tc), F32),
            pltpu.VMEM((seq_len, tc), F32),
        ],
        compiler_params=_compiler_params(("arbitrary", "arbitrary")),
        name="filter_spec",
    )(feats_t, w1p, w2, vecs, w3, w3, t, deltas, fwd, w_in)


def _short_conv(src_ref, pad_ref, w_ref, b_ref, dst_ref):
    n = src_ref.shape[0]
    q = n // ROW_PHASES
    band = jnp.zeros((SUBLANES, LANES), F32)
    for s in range(pad_ref.shape[0]):
        lanes = slice(s * LANES, (s + 1) * LANES)
        pad_ref[s, 0:SUBLANES, :] = band
        pad_ref[s, n + SUBLANES:n + 2 * SUBLANES, :] = band
        pad_ref[s, SUBLANES:n + SUBLANES, :] = src_ref[:, lanes].astype(F32)
        w0 = w_ref[0:1, lanes]
        w1 = w_ref[1:2, lanes]
        w2 = w_ref[2:3, lanes]
        bias = b_ref[:, lanes]
        for k in range(ROW_PHASES):
            base = SUBLANES + k
            prev = pad_ref[s, pl.ds(base - 1, q, stride=ROW_PHASES), :]
            cur = pad_ref[s, pl.ds(base, q, stride=ROW_PHASES), :]
            nxt = pad_ref[s, pl.ds(base + 1, q, stride=ROW_PHASES), :]
            dst_ref[s, pl.ds(k, q, stride=ROW_PHASES), :] = (
                prev * w0 + cur * w1 + nxt * w2 + bias)


def _toeplitz_product(z, coef):
    add = lambda x, y: (x[0] + y[0], x[1] + y[1])
    mul = lambda g, x: (x[0] * g[0] - x[1] * g[1], x[0] * g[1] + x[1] * g[0])

    def inner(n0, p, q):
        base = mul(coef(n0), add(p, q))
        return add(base, mul(coef(n0 + 1), q)), add(base, mul(coef(n0 + 2), p))

    s0 = add(z[0], z[2])
    s1 = add(z[1], z[3])
    a0, a1 = inner(0, s0, s1)
    b0, b1 = inner(3, z[2], z[3])
    y0 = add(a0, b0)
    y1 = add(a1, b1)
    c0, c1 = inner(6, z[0], z[1])
    return [y0, y1, add(a0, c0), add(a1, c1)]


def _hyena_conv_kernel(z_ref, g_ref, cwz_ref, cbz_ref, cwg_ref, cbg_ref, skip_ref,
                       fwd_ref, inv_ref, kr_ref, ki_ref, w32_ref, o_ref, w16_ref, zf_ref,
                       gf_ref, zs_ref, pad_ref, spec_ref, *, conv_z):
    _cast_slices((w32_ref,), (w16_ref,))
    for bb in range(z_ref.shape[0]):
        _conv_sequence(z_ref.at[bb], g_ref.at[bb], cwz_ref, cbz_ref, cwg_ref, cbg_ref,
                       skip_ref, fwd_ref, inv_ref, kr_ref, ki_ref, o_ref.at[bb], zf_ref,
                       gf_ref, zs_ref, pad_ref, spec_ref, conv_z)


def _conv_sequence(z_ref, g_ref, cwz_ref, cbz_ref, cwg_ref, cbg_ref, skip_ref, fwd_ref,
                   inv_ref, kr_ref, ki_ref, o_ref, zf_ref, gf_ref, zs_ref, pad_ref, spec_ref,
                   conv_z):
    p = CONV_BLOCK
    nb = z_ref.shape[0] // p
    n_slabs = pad_ref.shape[0]
    if conv_z:
        _short_conv(z_ref, pad_ref, cwz_ref, cbz_ref, zf_ref)
    _short_conv(g_ref, pad_ref, cwg_ref, cbg_ref, gf_ref)

    fwd = fwd_ref[...]
    for j in range(nb):
        rows = slice(j * p, (j + 1) * p)
        if conv_z:
            zb = jnp.concatenate([zf_ref[s, rows, :] for s in range(n_slabs)],
                                 axis=1).astype(BF16)
        else:
            zb = z_ref[rows, :]
        zs_ref[j] = jnp.dot(fwd, zb, preferred_element_type=F32)

    parts = [[] for _ in range(nb)]
    for kc in range(p // SPEC_CHUNK):
        for r in range(0, SPEC_CHUNK, SPEC_ROWS):
            slot = kc * SPEC_CHUNK + r
            re_in = slice(slot, slot + SPEC_ROWS)
            im_in = slice(p + slot, p + slot + SPEC_ROWS)
            for s in range(n_slabs):
                lanes = slice(s * LANES, (s + 1) * LANES)
                z = [(zs_ref[j, re_in, lanes], zs_ref[j, im_in, lanes]) for j in range(nb)]
                ys = _toeplitz_product(
                    z, lambda n: (kr_ref[n, re_in, lanes], ki_ref[n, re_in, lanes]))
                for i in range(nb):
                    spec_ref[kc, i, r:r + SPEC_ROWS, lanes] = ys[i][0].astype(BF16)
                    spec_ref[kc, i, SPEC_CHUNK + r:SPEC_CHUNK + r + SPEC_ROWS, lanes] = (
                        ys[i][1].astype(BF16))
        for i in range(nb):
            parts[i].append(jnp.dot(inv_ref[kc], spec_ref[kc, i],
                                    preferred_element_type=F32))

    for i in range(nb):
        y = functools.reduce(lambda a, b: a + b, parts[i])
        rows = slice(i * p, (i + 1) * p)
        for s in range(n_slabs):
            lanes = slice(s * LANES, (s + 1) * LANES)
            z_blk = zf_ref[s, rows, :] if conv_z else z_ref[rows, lanes].astype(F32)
            o_ref[rows, lanes] = (gf_ref[s, rows, :]
                                  * (y[:, lanes] + z_blk * skip_ref[:, lanes])
                                  ).astype(o_ref.dtype)


def _hyena_conv(z_arr, z_col, g_arr, g_col, conv_w, conv_b, skip, fwd, inv, kr, ki, order,
                conv_z, later_weight):
    _, bsz, seq_len, _ = z_arr.shape
    tc = CONV_CH_TILE
    n_ct = B_WIDTH // tc
    n_off = N_COEFS
    nb = seq_len // CONV_BLOCK
    n_slabs = tc // LANES
    cwz_col = z_col if conv_z else 0
    kernel = functools.partial(_hyena_conv_kernel, conv_z=conv_z)
    n_bt = bsz // CONV_SEQS
    cast_in, cast_out, cast_shapes = _cast_specs([later_weight], n_ct * n_bt,
                                                 lambda c, b: c * n_bt + b)
    seq_block = (None, CONV_SEQS, seq_len, tc)
    return pl.pallas_call(
        kernel,
        grid=(n_ct, n_bt),
        in_specs=[
            pl.BlockSpec(seq_block, lambda c, b: (z_col + c, b, 0, 0)),
            pl.BlockSpec(seq_block, lambda c, b: (g_col + c, b, 0, 0)),
            pl.BlockSpec((SHORT_CONV, tc), lambda c, b: (0, cwz_col + c)),
            pl.BlockSpec((1, tc), lambda c, b: (0, cwz_col + c)),
            pl.BlockSpec((SHORT_CONV, tc), lambda c, b: (0, g_col + c)),
            pl.BlockSpec((1, tc), lambda c, b: (0, g_col + c)),
            pl.BlockSpec((None, 1, tc), lambda c, b: (order, 0, c)),
            pl.BlockSpec((2 * CONV_BLOCK, CONV_BLOCK), lambda c, b: (0, 0)),
            pl.BlockSpec(inv.shape, lambda c, b: (0, 0, 0)),
            pl.BlockSpec((None, None, n_off, CONV_BLOCK, tc),
                         lambda c, b: (order, c, 0, 0, 0)),
            pl.BlockSpec((None, None, n_off, CONV_BLOCK, tc),
                         lambda c, b: (order, c, 0, 0, 0)),
        ] + cast_in,
        out_specs=[pl.BlockSpec(seq_block, lambda c, b: (c, b, 0, 0))]
        + cast_out,
        out_shape=[jax.ShapeDtypeStruct((n_ct, bsz, seq_len, tc), BF16)] + cast_shapes,
        scratch_shapes=[
            pltpu.VMEM((n_slabs, seq_len, LANES), F32),
            pltpu.VMEM((n_slabs, seq_len, LANES), F32),
            pltpu.VMEM((nb, 2 * CONV_BLOCK, tc), F32),
            pltpu.VMEM((n_slabs, seq_len + 2 * SUBLANES, LANES), F32),
            pltpu.VMEM((CONV_BLOCK // SPEC_CHUNK, nb, 2 * SPEC_CHUNK, tc), BF16),
        ],
        compiler_params=_compiler_params(("parallel", "parallel")),
        name="hyena_conv%d" % order,
    )(z_arr, g_arr, conv_w, conv_b, conv_w, conv_b, skip, fwd, inv, kr, ki, later_weight)


def _mix_ffn_kernel(u_ref, v_ref, yb_ref, gate_ref, x_ref, wcat_ref, sbias_ref, woa_ref,
                    wob_ref, wo_ref, gmix_ref, gpre_ref, w1_ref, w2_ref, gpost_ref, o_ref,
                    ya_ref):
    n_pairs = A_WIDTH // LANES
    lane = lax.broadcasted_iota(jnp.int32, (CHUNK, LANES), 1)
    low_half = lane < A_GROUP_DIM
    zero = jnp.zeros((CHUNK, LANES), BF16)
    for sub in _sub_tiles(u_ref.shape[0], MIX_SUB_TILE):
        for c in range(sub.start, sub.stop, CHUNK):
            rows = slice(c, c + CHUNK)
            parts = []
            for j in range(n_pairs):
                vt = v_ref[rows, j * LANES:(j + 1) * LANES]
                rhs = jnp.concatenate([jnp.where(low_half, vt, zero),
                                       jnp.where(low_half, zero, vt)], axis=0)
                parts.append(jnp.dot(wcat_ref[j], rhs, preferred_element_type=F32))
            s = jnp.concatenate(parts, axis=1) + sbias_ref[...]
            ya_ref[rows, :] = (u_ref[rows, :].astype(F32) * s).astype(BF16)

        ya = jnp.dot(ya_ref[sub, :], woa_ref[...], preferred_element_type=F32)
        yb_in = jnp.concatenate([yb_ref[k, sub, :] for k in range(yb_ref.shape[0])], axis=1)
        yb = jnp.dot(yb_in, wob_ref[...], preferred_element_type=F32)
        ga = gate_ref[sub, :D_MODEL].astype(F32)
        gb = gate_ref[sub, D_MODEL:].astype(F32)
        mix = (ga * ya + gb * yb).astype(BF16)
        m = jnp.dot(mix, wo_ref[...], preferred_element_type=F32)
        h = x_ref[sub, :] + m * _rms_scale(m) * gmix_ref[...]

        hn = (h * _rms_scale(h) * gpre_ref[...]).astype(BF16)
        acc = None
        for k in range(D_FF // FF_CHUNK):
            cols = slice(k * FF_CHUNK, (k + 1) * FF_CHUNK)
            f = jnp.dot(hn, w1_ref[:, cols], preferred_element_type=F32)
            f = jnp.square(jnp.maximum(f, 0.0)).astype(BF16)
            part = jnp.dot(f, w2_ref[cols, :], preferred_element_type=F32)
            acc = part if acc is None else acc + part
        o_ref[sub, :] = h + acc * _rms_scale(acc) * gpost_ref[...]


def _mix_ffn(u, v, yb, gates, x2, wcat, sbias, w_out_a, w_out_b, w_o, g_post_mix, g_pre_ffn,
             w_ff1, w_ff2, g_post_ffn):
    m = x2.shape[0]
    tm = MIX_TILE
    row = lambda i: (i, 0)
    fixed2 = lambda i: (0, 0)
    fixed3 = lambda i: (0, 0, 0)
    gain = _resident((1, D_MODEL), fixed2)
    return pl.pallas_call(
        _mix_ffn_kernel,
        grid=(m // tm,),
        in_specs=[
            pl.BlockSpec((tm, A_WIDTH), row),
            pl.BlockSpec((tm, A_WIDTH), row),
            pl.BlockSpec((yb.shape[0], tm, yb.shape[2]), lambda i: (0, i, 0)),
            pl.BlockSpec((tm, 2 * D_MODEL), row),
            pl.BlockSpec((tm, D_MODEL), row),
            _resident(wcat.shape, fixed3),
            _resident(sbias.shape, fixed2),
            _resident(w_out_a.shape, fixed2),
            _resident(w_out_b.shape, fixed2),
            _resident(w_o.shape, fixed2),
            gain,
            gain,
            _resident(w_ff1.shape, fixed2),
            _resident(w_ff2.shape, fixed2),
            gain,
        ],
        out_specs=pl.BlockSpec((tm, D_MODEL), row),
        out_shape=jax.ShapeDtypeStruct((m, D_MODEL), F32),
        scratch_shapes=[pltpu.VMEM((tm, A_WIDTH), BF16)],
        compiler_params=_compiler_params(("parallel",)),
        name="mix_ffn",
    )(u, v, yb, gates, x2, wcat, sbias, w_out_a, w_out_b, w_o, g_post_mix, g_pre_ffn,
      w_ff1, w_ff2, g_post_ffn)


def _layer(h2, bsz, seq_len, p):
    fwd, inv = _dft_matrices()
    kr, ki, w_in = _filter_spectra(seq_len, p["b_filt_w1"], p["b_filt_b1"], p["b_filt_f1"],
                                   p["b_filt_w2"], p["b_filt_b2"], p["b_filt_f2"],
                                   p["b_filt_w3"], fwd, p["w_in"])
    (u, v, pb, gates), (w_out_a, w_out_b, w_o) = _in_proj(
        h2, p["g_pre_mix"][None, :], w_in, p["a_v_gain"][None, :],
        [p["w_out_a"], p["w_out_b"], p["w_o"]])

    pb3 = pb.reshape(pb.shape[0], bsz, seq_len, CONV_CH_TILE)
    conv_w = p["b_conv_w"]
    conv_b = p["b_conv_b"][None, :]
    skip = p["b_skip"][:, None, :]
    per = B_WIDTH // CONV_CH_TILE
    z1, w_ff1 = _hyena_conv(pb3, 2 * per, pb3, 0, conv_w, conv_b, skip, fwd, inv, kr, ki, 0,
                            True, p["w_ff1"])
    z2, w_ff2 = _hyena_conv(z1, 0, pb3, per, conv_w, conv_b, skip, fwd, inv, kr, ki, 1,
                            False, p["w_ff2"])
    yb = z2.reshape(per, bsz * seq_len, CONV_CH_TILE)

    w_s = p["a_w_s"]
    n_pairs = A_GROUPS // 2
    wcat = jnp.concatenate([w_s[0::2], w_s[1::2]], axis=2).astype(BF16)
    assert wcat.shape == (n_pairs, CHUNK, 2 * CHUNK)
    sbias = jnp.repeat(p["a_b_s"].T, A_GROUP_DIM, axis=1)

    return _mix_ffn(u, v, yb, gates, h2, wcat, sbias, w_out_a, w_out_b, w_o,
                    p["g_post_mix"][None, :], p["g_pre_ffn"][None, :], w_ff1, w_ff2,
                    p["g_post_ffn"][None, :])


def kernel(x, g_pre_mix, w_in, a_v_gain, a_w_s, a_b_s, w_out_a, b_conv_w, b_conv_b,
           b_filt_w1, b_filt_b1, b_filt_f1, b_filt_w2, b_filt_b2, b_filt_f2, b_filt_w3,
           b_skip, w_out_b, w_o, g_post_mix, g_pre_ffn, w_ff1, w_ff2, g_post_ffn):
    params = dict(g_pre_mix=g_pre_mix, w_in=w_in, a_v_gain=a_v_gain, a_w_s=a_w_s, a_b_s=a_b_s,
                  w_out_a=w_out_a, b_conv_w=b_conv_w, b_conv_b=b_conv_b, b_filt_w1=b_filt_w1,
                  b_filt_b1=b_filt_b1, b_filt_f1=b_filt_f1, b_filt_w2=b_filt_w2,
                  b_filt_b2=b_filt_b2, b_filt_f2=b_filt_f2, b_filt_w3=b_filt_w3, b_skip=b_skip,
                  w_out_b=w_out_b, w_o=w_o, g_post_mix=g_post_mix, g_pre_ffn=g_pre_ffn,
                  w_ff1=w_ff1, w_ff2=w_ff2, g_post_ffn=g_post_ffn)
    bsz, seq_len, d = x.shape
    assert d == D_MODEL and seq_len % CONV_BLOCK == 0
    assert seq_len // CONV_BLOCK == N_BLOCKS
    assert (bsz * seq_len) % TOKEN_TILE == 0 and seq_len % ROW_PHASES == 0
    h2 = x.reshape(bsz * seq_len, d)
    for i in range(g_pre_mix.shape[0]):
        h2 = _layer(h2, bsz, seq_len, {k: v[i] for k, v in params.items()})
    return h2.reshape(bsz, seq_len, d)
```

```python
import functools
import math

import jax
import jax.numpy as jnp
import numpy as np
from jax import lax
from jax.experimental import pallas as pl
from jax.experimental.pallas import tpu as pltpu

F32 = jnp.float32
BF16 = jnp.bfloat16

D_MODEL = 1024
A_WIDTH = 512
A_GROUPS = 8
A_GROUP_DIM = A_WIDTH // A_GROUPS
CHUNK = 128
B_WIDTH = 512
HYENA_ORDER = 2
SHORT_CONV = 3
FILTER_EMB = 33
FILTER_HIDDEN = 64
DECAY_TARGET = 1e-2
FAST_DECAY_PCT = 0.3
SLOW_DECAY_PCT = 1.5
DECAY_SHIFT = 0.05
D_FF = 4 * D_MODEL
EPS = 1e-6

LANES = 128
SUBLANES = 8
CONV_BLOCK = 512
N_BLOCKS = 4
N_COEFS = 3 ** (N_BLOCKS // 2)
TOKEN_TILE = 1024
MIX_TILE = 512
MIX_SUB_TILE = 512
SUB_TILE = 512
HALO = 16
CONV_CH_TILE = 256
CONV_SEQS = 1
SPEC_CHUNK = 128
SPEC_ROWS = 16
FF_CHUNK = 1024
VMEM_LIMIT = 56 * 1024 * 1024


def _compiler_params(semantics, flags=None):
    return pltpu.CompilerParams(dimension_semantics=semantics,
                                vmem_limit_bytes=VMEM_LIMIT, flags=flags)


def _resident(block_shape, index_map):
    return pl.BlockSpec(block_shape, index_map, pipeline_mode=pl.Buffered(1))


def _rms_scale(x):
    return lax.rsqrt(jnp.mean(x * x, axis=-1, keepdims=True) + EPS)


def _gelu_tanh(x):
    c = math.sqrt(2.0 / math.pi)
    return x * (0.5 * (1.0 + jnp.tanh(c * (x + 0.044715 * (x * x * x)))))


def _cast_specs(weights, n_steps, step_of):
    in_specs, out_specs, out_shapes = [], [], []
    for w in weights:
        rows, rem = divmod(w.shape[0], n_steps)
        assert rem == 0 and rows % (2 * SUBLANES) == 0
        spec = pl.BlockSpec((rows, w.shape[1]), lambda *idx: (step_of(*idx), 0))
        in_specs.append(spec)
        out_specs.append(spec)
        out_shapes.append(jax.ShapeDtypeStruct(w.shape, BF16))
    return in_specs, out_specs, out_shapes


def _cast_slices(src_refs, dst_refs):
    for src, dst in zip(src_refs, dst_refs):
        dst[...] = src[...].astype(BF16)


def _sub_tiles(n_rows, sub=SUB_TILE):
    return [slice(s, s + sub) for s in range(0, n_rows, sub)]


def _in_proj_kernel(x_ref, xprev_ref, xnext_ref, g_ref, w_ref, vg_ref, cw_ref, cb_ref,
                    woa32_ref, wob32_ref, wo32_ref, u_ref, v_ref, pb_ref, gate_ref, woa_ref,
                    wob_ref, wo_ref, *, tiles_per_seq):
    _cast_slices((woa32_ref, wob32_ref, wo32_ref), (woa_ref, wob_ref, wo_ref))
    split_a = 2 * A_WIDTH
    split_b = split_a + (HYENA_ORDER + 1) * B_WIDTH
    normed = lambda x: (x * _rms_scale(x) * g_ref[...]).astype(BF16)
    subs = _sub_tiles(x_ref.shape[0])
    last = len(subs) - 1
    xns = [normed(x_ref[rows, :]) for rows in subs]

    pbs = []
    for n, xn in enumerate(xns):
        lhs = (([normed(xprev_ref[...])] if n == 0 else []) + [xn]
               + ([normed(xnext_ref[...])] if n == last else []))
        lhs = jnp.concatenate(lhs, axis=0) if len(lhs) > 1 else xn
        pbs.append(jnp.dot(lhs, w_ref[:, split_a:split_b], preferred_element_type=F32))

    for rows, xn in zip(subs, xns):
        pg = jnp.dot(xn, w_ref[:, split_b:], preferred_element_type=F32)
        gate_ref[rows, :] = jax.nn.sigmoid(pg).astype(BF16)

        pa = _gelu_tanh(jnp.dot(xn, w_ref[:, :split_a], preferred_element_type=F32))
        u_ref[rows, :] = pa[:, :A_WIDTH].astype(BF16)
        v = pa[:, A_WIDTH:]
        mu = jnp.mean(v, axis=-1, keepdims=True)
        vc = v - mu
        var = jnp.mean(vc * vc, axis=-1, keepdims=True)
        v_ref[rows, :] = (vc * lax.rsqrt(var + EPS) * vg_ref[...]).astype(BF16)

    tile = pl.program_id(0) % tiles_per_seq
    at_start = tile == 0
    at_end = tile == tiles_per_seq - 1
    w0 = cw_ref[0:1, :]
    w1 = cw_ref[1:2, :]
    w2 = cw_ref[2:3, :]
    sub = SUB_TILE
    row_id = lax.broadcasted_iota(jnp.int32, (sub, 1), 0)
    first_row = lambda n: HALO if n == 0 else 0
    for n, rows in enumerate(subs):
        r0 = first_row(n)
        cur = pbs[n][r0:r0 + sub]
        if n == 0:
            before = jnp.where(at_start, 0.0, pbs[0][HALO - 1:HALO])
        else:
            before = pbs[n - 1][first_row(n - 1) + sub - 1:first_row(n - 1) + sub]
        if n == last:
            after = jnp.where(at_end, 0.0, pbs[n][r0 + sub:r0 + sub + 1])
        else:
            after = pbs[n + 1][0:1]
        prev = jnp.where(row_id == 0, before, pltpu.roll(cur, 1, 0))
        nxt = jnp.where(row_id == sub - 1, after, pltpu.roll(cur, sub - 1, 0))
        pc = prev * w0 + cur * w1 + nxt * w2 + cb_ref[...]
        for k in range(pb_ref.shape[0]):
            pb_ref[k, rows, :] = pc[:, k * CONV_CH_TILE:(k + 1) * CONV_CH_TILE].astype(BF16)


def _in_proj(x2, seq_len, g_pre, w_in, v_gain, conv_w, conv_b, later_weights):
    m = x2.shape[0]
    in_cols = w_in.shape[1]
    tm = TOKEN_TILE
    pb_tiles = (HYENA_ORDER + 1) * B_WIDTH // CONV_CH_TILE
    gate_cols = 2 * D_MODEL
    row = lambda i: (i, 0)
    fixed = lambda i: (0, 0)
    cast_in, cast_out, cast_shapes = _cast_specs(later_weights, m // tm, lambda i: i)
    halos_per_tile = tm // HALO
    n_halo_blocks = m // HALO
    before = lambda i: (jnp.maximum(i * halos_per_tile - 1, 0), 0)
    after = lambda i: (jnp.minimum((i + 1) * halos_per_tile, n_halo_blocks - 1), 0)
    assert seq_len % tm == 0
    kernel = functools.partial(_in_proj_kernel, tiles_per_seq=seq_len // tm)
    outs = pl.pallas_call(
        kernel,
        grid=(m // tm,),
        in_specs=[
            pl.BlockSpec((tm, D_MODEL), row),
            pl.BlockSpec((HALO, D_MODEL), before),
            pl.BlockSpec((HALO, D_MODEL), after),
            _resident((1, D_MODEL), fixed),
            _resident((D_MODEL, in_cols), fixed),
            _resident((1, A_WIDTH), fixed),
            _resident(conv_w.shape, fixed),
            _resident(conv_b.shape, fixed),
        ] + cast_in,
        out_specs=[
            pl.BlockSpec((tm, A_WIDTH), row),
            pl.BlockSpec((tm, A_WIDTH), row),
            pl.BlockSpec((pb_tiles, tm, CONV_CH_TILE), lambda i: (0, i, 0)),
            pl.BlockSpec((tm, gate_cols), row),
        ] + cast_out,
        out_shape=[
            jax.ShapeDtypeStruct((m, A_WIDTH), BF16),
            jax.ShapeDtypeStruct((m, A_WIDTH), BF16),
            jax.ShapeDtypeStruct((pb_tiles, m, CONV_CH_TILE), BF16),
            jax.ShapeDtypeStruct((m, gate_cols), BF16),
        ] + cast_shapes,
        compiler_params=_compiler_params(("parallel",)),
        name="in_proj",
    )(x2, x2, x2, g_pre, w_in, v_gain, conv_w, conv_b, *later_weights)
    return outs[:4], outs[4:]


def _dft_matrices():
    p = CONV_BLOCK
    n = 4 * p
    f = np.arange(p, dtype=np.int64)[:, None]
    t = np.arange(p, dtype=np.int64)[None, :]
    theta = (((2 * f + 1) * t) % n).astype(np.float64) * (2.0 * math.pi / n)
    c, s = np.cos(theta), np.sin(theta)
    fwd = np.concatenate([c, -s], axis=0).astype(np.float32)
    chunks = [np.concatenate([c.T[:, k:k + SPEC_CHUNK], -s.T[:, k:k + SPEC_CHUNK]], axis=1)
              for k in range(0, p, SPEC_CHUNK)]
    inv = (np.stack(chunks) * (1.0 / p)).astype(np.float32)
    return jnp.asarray(fwd).astype(BF16), jnp.asarray(inv).astype(BF16)


def _filter_spec_kernel(feats_t_ref, w1_ref, w2_ref, vecs_ref, w3f_ref, w3b_ref, t_ref,
                        deltas_ref, fwd_ref, win32_ref, kr_ref, ki_ref, win_ref, h2t_ref,
                        hf_ref, hb_ref):
    _cast_slices((win32_ref,), (win_ref,))
    p = CONV_BLOCK
    nb = hf_ref.shape[0] // p
    hp = lax.Precision.HIGHEST

    over_rows = (((0,), (0,)), ((), ()))

    @pl.when((pl.program_id(0) == 0) & (pl.program_id(1) == 0))
    def _():
        b1, f1, b2, f2 = (vecs_ref[:, k:k + 1] for k in range(4))
        a = jnp.sin(f1 * (lax.dot_general(w1_ref[...], feats_t_ref[...], over_rows,
                                          precision=hp, preferred_element_type=F32) + b1))
        h2t_ref[...] = jnp.sin(f2 * (lax.dot_general(w2_ref[...], a, over_rows, precision=hp,
                                                     preferred_element_type=F32) + b2))

    h2t = h2t_ref[...]
    h_hi = h2t.astype(BF16)
    h_lo = (h2t - h_hi.astype(F32)).astype(BF16)
    h_stack = jnp.concatenate([h_hi, h_hi, h_lo], axis=0)

    def last(w_ref):
        w = w_ref[...]
        w_hi = w.astype(BF16)
        w_lo = (w - w_hi.astype(F32)).astype(BF16)
        w_stack = jnp.concatenate([w_hi, w_lo, w_hi], axis=0)
        return lax.dot_general(h_stack, w_stack, over_rows, preferred_element_type=F32)

    window = jnp.exp(-t_ref[...] * deltas_ref[...]) + DECAY_SHIFT
    hf = last(w3f_ref) * window
    hb = last(w3b_ref) * window
    ss = (jnp.sum(hf * hf, axis=0, keepdims=True)
          + jnp.sum(hb * hb, axis=0, keepdims=True))
    r = lax.rsqrt(ss + EPS)
    hf_ref[...] = hf * r
    hb_ref[...] = hb * r

    fwd = fwd_ref[...]
    freq = lax.broadcasted_iota(jnp.int32, (p, hf_ref.shape[1]), 0)
    sign = jnp.where((freq & 1) == 0, 1.0, -1.0)

    def block_spectra(h_ref):
        out = []
        for m in range(nb):
            blk = h_ref[m * p:(m + 1) * p, :]
            s = jnp.dot(fwd, blk.astype(BF16), preferred_element_type=F32)
            out.append((s[:p], s[p:], blk[0:1, :]))
        return out

    sf = block_spectra(hf_ref)
    sb = block_spectra(hb_ref)
    k = {0: (sf[0][0] + sb[0][0], sf[0][1] - sb[0][1])}
    for d in range(1, nb):
        k[d] = (sf[d][0] - sign * sf[d - 1][1],
                sf[d][1] + sign * (sf[d - 1][0] - sf[d - 1][2]))
        k[-d] = (sb[d][0] - sign * sb[d - 1][1],
                 -(sb[d][1] + sign * (sb[d - 1][0] - sb[d - 1][2])))

    sub = lambda x, y: (x[0] - y[0], x[1] - y[1])
    blocks = [(k[0], k[-1], k[1])]
    if nb == 4:
        blocks += [(sub(k[-2], k[0]), sub(k[-3], k[-1]), sub(k[-1], k[1])),
                   (sub(k[2], k[0]), sub(k[1], k[-1]), sub(k[3], k[1]))]
    n = 0
    for a, b, c in blocks:
        for coef in (a, sub(b, a), sub(c, a)):
            kr_ref[n] = coef[0]
            ki_ref[n] = coef[1]
            n += 1


def _filter_spectra(seq_len, w1, b1, f1, w2, b2, f2, w3, fwd, w_in):
    t = np.linspace(0.0, 1.0, seq_len)[:, None]
    bands = (FILTER_EMB - 1) // 2
    w = 2.0 * math.pi * np.arange(seq_len)[:, None] / seq_len
    fr = np.linspace(1e-4, bands - 1, bands)[None, :]
    feats = np.concatenate([t, np.cos(fr * w), -np.sin(fr * w)], axis=-1)
    feats_t = np.pad(feats, ((0, 0), (0, LANES - FILTER_EMB))).T.astype(np.float32)
    max_decay = math.log(DECAY_TARGET) / FAST_DECAY_PCT
    min_decay = math.log(DECAY_TARGET) / SLOW_DECAY_PCT
    deltas = np.abs(np.linspace(min_decay, max_decay, B_WIDTH))[None, :].astype(np.float32)
    t = t.astype(np.float32)
    w1p = jnp.pad(w1, ((0, LANES - FILTER_EMB), (0, 0)))
    vecs = jnp.stack([b1, f1, b2, f2], axis=1)

    tc = CONV_CH_TILE
    n_ct = B_WIDTH // tc
    n_off = N_COEFS
    fixed = lambda o, c: (0, 0)
    out_sds = jax.ShapeDtypeStruct((HYENA_ORDER, n_ct, n_off, CONV_BLOCK, tc), F32)
    out_spec = pl.BlockSpec((None, None, n_off, CONV_BLOCK, tc),
                            lambda o, c: (o, c, 0, 0, 0))
    cast_in, cast_out, cast_shapes = _cast_specs([w_in], HYENA_ORDER * n_ct,
                                                 lambda o, c: o * n_ct + c)
    return pl.pallas_call(
        _filter_spec_kernel,
        grid=(HYENA_ORDER, n_ct),
        in_specs=[
            _resident((LANES, seq_len), fixed),
            _resident((LANES, FILTER_HIDDEN), fixed),
            _resident((FILTER_HIDDEN, FILTER_HIDDEN), fixed),
            _resident((FILTER_HIDDEN, 4), fixed),
            pl.BlockSpec((FILTER_HIDDEN, tc), lambda o, c: (0, (2 * o) * n_ct + c)),
            pl.BlockSpec((FILTER_HIDDEN, tc), lambda o, c: (0, (2 * o + 1) * n_ct + c)),
            _resident((seq_len, 1), fixed),
            pl.BlockSpec((1, tc), lambda o, c: (0, c)),
            _resident((2 * CONV_BLOCK, CONV_BLOCK), fixed),
        ] + cast_in,
        out_specs=[out_spec, out_spec] + cast_out,
        out_shape=[out_sds, out_sds] + cast_shapes,
        scratch_shapes=[
            pltpu.VMEM((FILTER_HIDDEN, seq_len), F32),
            pltpu.VMEM((seq_len, tc), F32),
            pltpu.VMEM((seq_len, tc), F32),
        ],
        compiler_params=_compiler_params(("arbitrary", "arbitrary")),
        name="filter_spec",
    )(feats_t, w1p, w2, vecs, w3, w3, t, deltas, fwd, w_in)


def _toeplitz_product(z, coef):
    add = lambda x, y: (x[0] + y[0], x[1] + y[1])
    mul = lambda g, x: (x[0] * g[0] - x[1] * g[1], x[0] * g[1] + x[1] * g[0])

    def inner(n0, p, q):
        base = mul(coef(n0), add(p, q))
        return add(base, mul(coef(n0 + 1), q)), add(base, mul(coef(n0 + 2), p))

    if len(z) == 2:
        return list(inner(0, z[0], z[1]))
    s0 = add(z[0], z[2])
    s1 = add(z[1], z[3])
    a0, a1 = inner(0, s0, s1)
    b0, b1 = inner(3, z[2], z[3])
    y0 = add(a0, b0)
    y1 = add(a1, b1)
    c0, c1 = inner(6, z[0], z[1])
    return [y0, y1, add(a0, c0), add(a1, c1)]


def _hyena_conv_kernel(z_ref, g_ref, skip_ref, fwd_ref, inv_ref, kr_ref, ki_ref, w32_ref,
                       o_ref, w16_ref, zs_ref, spec_ref):
    _cast_slices((w32_ref,), (w16_ref,))
    for bb in range(z_ref.shape[0]):
        _conv_sequence(z_ref.at[bb], g_ref.at[bb], skip_ref, fwd_ref, inv_ref, kr_ref, ki_ref,
                       o_ref.at[bb], zs_ref, spec_ref)


def _conv_sequence(z_ref, g_ref, skip_ref, fwd_ref, inv_ref, kr_ref, ki_ref, o_ref, zs_ref,
                   spec_ref):
    p = CONV_BLOCK
    nb = z_ref.shape[0] // p
    n_slabs = z_ref.shape[1] // LANES

    fwd = fwd_ref[...]
    for j in range(nb):
        zs_ref[j] = jnp.dot(fwd, z_ref[j * p:(j + 1) * p, :], preferred_element_type=F32)

    parts = [[] for _ in range(nb)]
    for kc in range(p // SPEC_CHUNK):
        for r in range(0, SPEC_CHUNK, SPEC_ROWS):
            slot = kc * SPEC_CHUNK + r
            re_in = slice(slot, slot + SPEC_ROWS)
            im_in = slice(p + slot, p + slot + SPEC_ROWS)
            for s in range(n_slabs):
                lanes = slice(s * LANES, (s + 1) * LANES)
                z = [(zs_ref[j, re_in, lanes], zs_ref[j, im_in, lanes]) for j in range(nb)]
                ys = _toeplitz_product(
                    z, lambda n: (kr_ref[n, re_in, lanes], ki_ref[n, re_in, lanes]))
                for i in range(nb):
                    spec_ref[kc, i, r:r + SPEC_ROWS, lanes] = ys[i][0].astype(BF16)
                    spec_ref[kc, i, SPEC_CHUNK + r:SPEC_CHUNK + r + SPEC_ROWS, lanes] = (
                        ys[i][1].astype(BF16))
        for i in range(nb):
            parts[i].append(jnp.dot(inv_ref[kc], spec_ref[kc, i],
                                    preferred_element_type=F32))

    for i in range(nb):
        y = functools.reduce(lambda a, b: a + b, parts[i])
        rows = slice(i * p, (i + 1) * p)
        for s in range(n_slabs):
            lanes = slice(s * LANES, (s + 1) * LANES)
            z_blk = z_ref[rows, lanes].astype(F32)
            o_ref[rows, lanes] = (g_ref[rows, lanes].astype(F32)
                                  * (y[:, lanes] + z_blk * skip_ref[:, lanes])
                                  ).astype(o_ref.dtype)


def _hyena_conv(z_arr, z_col, g_arr, g_col, skip, fwd, inv, kr, ki, order, later_weight):
    _, bsz, seq_len, _ = z_arr.shape
    tc = CONV_CH_TILE
    n_ct = B_WIDTH // tc
    nb = seq_len // CONV_BLOCK
    n_bt = bsz // CONV_SEQS
    cast_in, cast_out, cast_shapes = _cast_specs([later_weight], n_ct * n_bt,
                                                 lambda c, b: c * n_bt + b)
    seq_block = (None, CONV_SEQS, seq_len, tc)
    coef_block = (None, None, N_COEFS, CONV_BLOCK, tc)
    return pl.pallas_call(
        _hyena_conv_kernel,
        grid=(n_ct, n_bt),
        in_specs=[
            pl.BlockSpec(seq_block, lambda c, b: (z_col + c, b, 0, 0)),
            pl.BlockSpec(seq_block, lambda c, b: (g_col + c, b, 0, 0)),
            pl.BlockSpec((None, 1, tc), lambda c, b: (order, 0, c)),
            _resident((2 * CONV_BLOCK, CONV_BLOCK), lambda c, b: (0, 0)),
            _resident(inv.shape, lambda c, b: (0, 0, 0)),
            pl.BlockSpec(coef_block, lambda c, b: (order, c, 0, 0, 0)),
            pl.BlockSpec(coef_block, lambda c, b: (order, c, 0, 0, 0)),
        ] + cast_in,
        out_specs=[pl.BlockSpec(seq_block, lambda c, b: (c, b, 0, 0))] + cast_out,
        out_shape=[jax.ShapeDtypeStruct((n_ct, bsz, seq_len, tc), BF16)] + cast_shapes,
        scratch_shapes=[
            pltpu.VMEM((nb, 2 * CONV_BLOCK, tc), F32),
            pltpu.VMEM((CONV_BLOCK // SPEC_CHUNK, nb, 2 * SPEC_CHUNK, tc), BF16),
        ],
        compiler_params=_compiler_params(("parallel", "parallel")),
        name="hyena_conv%d" % order,
    )(z_arr, g_arr, skip, fwd, inv, kr, ki, later_weight)


def _mix_ffn_kernel(u_ref, v_ref, yb_ref, gate_ref, x_ref, wcat_ref, sbias_ref, woa_ref,
                    wob_ref, wo_ref, gmix_ref, gpre_ref, w1_ref, w2_ref, gpost_ref, o_ref,
                    ya_ref):
    n_pairs = A_WIDTH // LANES
    lane = lax.broadcasted_iota(jnp.int32, (CHUNK, LANES), 1)
    low_half = lane < A_GROUP_DIM
    zero = jnp.zeros((CHUNK, LANES), BF16)
    for sub in _sub_tiles(u_ref.shape[0], MIX_SUB_TILE):
        for c in range(sub.start, sub.stop, CHUNK):
            rows = slice(c, c + CHUNK)
            parts = []
            for j in range(n_pairs):
                vt = v_ref[rows, j * LANES:(j + 1) * LANES]
                rhs = jnp.concatenate([jnp.where(low_half, vt, zero),
                                       jnp.where(low_half, zero, vt)], axis=0)
                parts.append(jnp.dot(wcat_ref[j], rhs, preferred_element_type=F32))
            s = jnp.concatenate(parts, axis=1) + sbias_ref[...]
            ya_ref[rows, :] = (u_ref[rows, :].astype(F32) * s).astype(BF16)

        ya = jnp.dot(ya_ref[sub, :], woa_ref[...], preferred_element_type=F32)
        yb_in = jnp.concatenate([yb_ref[k, sub, :] for k in range(yb_ref.shape[0])], axis=1)
        yb = jnp.dot(yb_in, wob_ref[...], preferred_element_type=F32)
        ga = gate_ref[sub, :D_MODEL].astype(F32)
        gb = gate_ref[sub, D_MODEL:].astype(F32)
        mix = (ga * ya + gb * yb).astype(BF16)
        m = jnp.dot(mix, wo_ref[...], preferred_element_type=F32)
        h = x_ref[sub, :] + m * _rms_scale(m) * gmix_ref[...]

        hn = (h * _rms_scale(h) * gpre_ref[...]).astype(BF16)
        acc = None
        for k in range(D_FF // FF_CHUNK):
            cols = slice(k * FF_CHUNK, (k + 1) * FF_CHUNK)
            f = jnp.dot(hn, w1_ref[:, cols], preferred_element_type=F32)
            f = jnp.square(jnp.maximum(f, 0.0)).astype(BF16)
            part = jnp.dot(f, w2_ref[cols, :], preferred_element_type=F32)
            acc = part if acc is None else acc + part
        o_ref[sub, :] = h + acc * _rms_scale(acc) * gpost_ref[...]


def _mix_ffn(u, v, yb, gates, x2, wcat, sbias, w_out_a, w_out_b, w_o, g_post_mix, g_pre_ffn,
             w_ff1, w_ff2, g_post_ffn):
    m = x2.shape[0]
    tm = MIX_TILE
    row = lambda i: (i, 0)
    fixed2 = lambda i: (0, 0)
    fixed3 = lambda i: (0, 0, 0)
    gain = _resident((1, D_MODEL), fixed2)
    return pl.pallas_call(
        _mix_ffn_kernel,
        grid=(m // tm,),
        in_specs=[
            pl.BlockSpec((tm, A_WIDTH), row),
            pl.BlockSpec((tm, A_WIDTH), row),
            pl.BlockSpec((yb.shape[0], tm, yb.shape[2]), lambda i: (0, i, 0)),
            pl.BlockSpec((tm, 2 * D_MODEL), row),
            pl.BlockSpec((tm, D_MODEL), row),
            _resident(wcat.shape, fixed3),
            _resident(sbias.shape, fixed2),
            _resident(w_out_a.shape, fixed2),
            _resident(w_out_b.shape, fixed2),
            _resident(w_o.shape, fixed2),
            gain,
            gain,
            _resident(w_ff1.shape, fixed2),
            _resident(w_ff2.shape, fixed2),
            gain,
        ],
        out_specs=pl.BlockSpec((tm, D_MODEL), row),
        out_shape=jax.ShapeDtypeStruct((m, D_MODEL), F32),
        scratch_shapes=[pltpu.VMEM((tm, A_WIDTH), BF16)],
        compiler_params=_compiler_params(("parallel",)),
        name="mix_ffn",
    )(u, v, yb, gates, x2, wcat, sbias, w_out_a, w_out_b, w_o, g_post_mix, g_pre_ffn,
      w_ff1, w_ff2, g_post_ffn)


def _layer(h2, bsz, seq_len, p):
    fwd, inv = _dft_matrices()
    kr, ki, w_in = _filter_spectra(seq_len, p["b_filt_w1"], p["b_filt_b1"], p["b_filt_f1"],
                                   p["b_filt_w2"], p["b_filt_b2"], p["b_filt_f2"],
                                   p["b_filt_w3"], fwd, p["w_in"])
    (u, v, pb, gates), (w_out_a, w_out_b, w_o) = _in_proj(
        h2, seq_len, p["g_pre_mix"][None, :], w_in, p["a_v_gain"][None, :], p["b_conv_w"],
        p["b_conv_b"][None, :], [p["w_out_a"], p["w_out_b"], p["w_o"]])

    pb4 = pb.reshape(pb.shape[0], bsz, seq_len, CONV_CH_TILE)
    skip = p["b_skip"][:, None, :]
    per = B_WIDTH // CONV_CH_TILE
    z1, w_ff1 = _hyena_conv(pb4, 2 * per, pb4, 0, skip, fwd, inv, kr, ki, 0, p["w_ff1"])
    z2, w_ff2 = _hyena_conv(z1, 0, pb4, per, skip, fwd, inv, kr, ki, 1, p["w_ff2"])
    yb = z2.reshape(per, bsz * seq_len, CONV_CH_TILE)

    w_s = p["a_w_s"]
    n_pairs = A_GROUPS // 2
    wcat = jnp.concatenate([w_s[0::2], w_s[1::2]], axis=2).astype(BF16)
    assert wcat.shape == (n_pairs, CHUNK, 2 * CHUNK)
    sbias = jnp.repeat(p["a_b_s"].T, A_GROUP_DIM, axis=1)

    return _mix_ffn(u, v, yb, gates, h2, wcat, sbias, w_out_a, w_out_b, w_o,
                    p["g_post_mix"][None, :], p["g_pre_ffn"][None, :], w_ff1, w_ff2,
                    p["g_post_ffn"][None, :])


def kernel(x, g_pre_mix, w_in, a_v_gain, a_w_s, a_b_s, w_out_a, b_conv_w, b_conv_b,
           b_filt_w1, b_filt_b1, b_filt_f1, b_filt_w2, b_filt_b2, b_filt_f2, b_filt_w3,
           b_skip, w_out_b, w_o, g_post_mix, g_pre_ffn, w_ff1, w_ff2, g_post_ffn):
    params = dict(g_pre_mix=g_pre_mix, w_in=w_in, a_v_gain=a_v_gain, a_w_s=a_w_s, a_b_s=a_b_s,
                  w_out_a=w_out_a, b_conv_w=b_conv_w, b_conv_b=b_conv_b, b_filt_w1=b_filt_w1,
                  b_filt_b1=b_filt_b1, b_filt_f1=b_filt_f1, b_filt_w2=b_filt_w2,
                  b_filt_b2=b_filt_b2, b_filt_f2=b_filt_f2, b_filt_w3=b_filt_w3, b_skip=b_skip,
                  w_out_b=w_out_b, w_o=w_o, g_post_mix=g_post_mix, g_pre_ffn=g_pre_ffn,
                  w_ff1=w_ff1, w_ff2=w_ff2, g_post_ffn=g_post_ffn)
    bsz, seq_len, d = x.shape
    assert d == D_MODEL and seq_len % CONV_BLOCK == 0
    assert seq_len // CONV_BLOCK == N_BLOCKS
    assert (bsz * seq_len) % TOKEN_TILE == 0
    h2 = x.reshape(bsz * seq_len, d)
    for i in range(g_pre_mix.shape[0]):
        h2 = _layer(h2, bsz, seq_len, {k: v[i] for k, v in params.items()})
    return h2.reshape(bsz, seq_len, d)
```
